```python
import math
import jax, jax.numpy as jnp
from jax import lax
import numpy as np

D_MODEL = 1024
BATCH = 16
SEQ = 2048
DEPTH = 4

GRID_W = 64
CTX_LEN = 256
N_MIXERS = 4
KIND_POOL = 0
KIND_CONV = 1
KIND_MLA = 2
KIND_DIFF = 3
ATTN_KINDS = (KIND_MLA, KIND_DIFF)
Q_BLOCK = 128
ROPE_BASE = 10000.0
NORM_EPS = 1e-6
D_FF = 4 * D_MODEL
POOL_WINDOWS = (2, 4, 8, 16)
POOL_GROUP = D_MODEL // len(POOL_WINDOWS)
CONV_WIDTH = 3
MLA_HEADS = D_MODEL // 128
MLA_NOPE = 128
MLA_ROPE = 64
MLA_V = 128
MLA_Q_LORA = 3 * D_MODEL // 8
MLA_KV_LORA = D_MODEL // 4
MLA_SCALE = (MLA_NOPE + MLA_ROPE) ** -0.5
DIFF_HEADS = D_MODEL // 128
DIFF_HEAD_DIM = D_MODEL // (2 * DIFF_HEADS)
DIFF_SCALE = DIFF_HEAD_DIM ** -0.5

kernel_name = "hybrid_interleaved_diffusion_trunk"


def rms_norm(x, g):
    xf = x.astype(jnp.float32)
    y = xf * lax.rsqrt(jnp.mean(xf * xf, axis=-1, keepdims=True) + NORM_EPS)
    return (y * g.astype(jnp.float32)).astype(x.dtype)


def modulate(h, shift, scale):
    return h * (1.0 + scale) + shift


def axial_rope_tables(rows, rot_dim):
    t = jnp.arange(rows * GRID_W)
    row = (t // GRID_W).astype(jnp.float32)
    col = (t % GRID_W).astype(jnp.float32)
    n = rot_dim // 2
    inv_freq = ROPE_BASE ** (-jnp.arange(0, n, 2, dtype=jnp.float32) / n)

    def table(pos):
        ang = pos[:, None] * inv_freq[None, :]
        ang = jnp.concatenate([ang, ang], axis=-1)
        return (jnp.cos(ang), jnp.sin(ang))

    return table(row) + table(col)


def _rotate(x, cos, sin):
    half = x.shape[-1] // 2
    x1, x2 = x[..., :half], x[..., half:]
    return x * cos + jnp.concatenate([-x2, x1], axis=-1) * sin


def apply_axial_rope(x, rope):
    cos_r, sin_r, cos_c, sin_c = rope
    extra = x.ndim - 3

    def bc(t):
        return t.reshape((1, t.shape[0]) + (1,) * extra + (t.shape[1],))

    n = x.shape[-1] // 2
    xf = x.astype(jnp.float32)
    out = jnp.concatenate([_rotate(xf[..., :n], bc(cos_r), bc(sin_r)),
                           _rotate(xf[..., n:], bc(cos_c), bc(sin_c))], axis=-1)
    return out.astype(x.dtype)


def sweep_query_blocks(fn, *qs):
    b, length = qs[0].shape[:2]
    nb = length // Q_BLOCK
    blocks = tuple(jnp.moveaxis(q.reshape((b, nb, Q_BLOCK) + q.shape[2:]), 1, 0) for q in qs)
    out = lax.map(lambda blk: fn(*blk), blocks)
    return jnp.moveaxis(out, 0, 1).reshape((b, length) + out.shape[3:])


def pool_mixer(h, w_grp, scale):
    b, length, _ = h.shape
    hf = h.astype(jnp.float32)
    cs = jnp.concatenate([jnp.zeros_like(hf[:, :1]), jnp.cumsum(hf, axis=1)], axis=1)
    t = jnp.arange(length)
    parts = []
    for g, w in enumerate(POOL_WINDOWS):
        lo = w // 2
        hi = w - 1 - lo
        start = jnp.clip(t - lo, 0, length)
        end = jnp.clip(t + hi + 1, 0, length)
        sl = slice(g * POOL_GROUP, (g + 1) * POOL_GROUP)
        csg = cs[..., sl]
        win_sum = jnp.take(csg, end, axis=1) - jnp.take(csg, start, axis=1)
        count = (end - start).astype(jnp.float32)[None, :, None]
        parts.append(win_sum / count - hf[..., sl])
    d = jnp.stack(parts, axis=2).astype(h.dtype)
    y = jnp.einsum('blgc,gcd->blgd', d, w_grp).reshape(b, length, D_MODEL)
    return y * scale


def short_conv_mixer(h, w_in, conv_w, w_out):
    length = h.shape[1]
    b_gate, c_gate, v = jnp.split(h @ w_in, 3, axis=-1)
    u = c_gate * v
    pad = CONV_WIDTH // 2
    up = jnp.pad(u, ((0, 0), (pad, pad), (0, 0)))
    z = up[:, 0:length] * conv_w[0]
    for k in range(1, CONV_WIDTH):
        z = z + up[:, k:k + length] * conv_w[k]
    return (b_gate * z) @ w_out


def mla_mixer(h, hc, w_down, g_q, g_kv, w_uq, w_ukv, w_o, rope, need_ctx):
    def parse_q(a_q, rot):
        cq = rms_norm(a_q, g_q)
        q = (cq @ w_uq).reshape(a_q.shape[:2] + (MLA_HEADS, MLA_NOPE + MLA_ROPE))
        qn, qr = q[..., :MLA_NOPE], q[..., MLA_NOPE:]
        if rot is not None:
            qr = apply_axial_rope(qr, rot)
        return qn, qr

    def parse_kv(a_kv, rot):
        ckv = rms_norm(a_kv[..., :MLA_KV_LORA], g_kv)
        kr = a_kv[..., MLA_KV_LORA:]
        if rot is not None:
            kr = apply_axial_rope(kr, rot)
        kv = (ckv @ w_ukv).reshape(a_kv.shape[:2] + (MLA_HEADS, MLA_NOPE + MLA_V))
        return kv[..., :MLA_NOPE], kr, kv[..., MLA_NOPE:]

    def attend(qn, qr, kn, kr, v):
        def blk(qn_b, qr_b):
            s = (jnp.einsum('bqhd,bkhd->bhqk', qn_b, kn, preferred_element_type=jnp.float32)
                 + jnp.einsum('bqhr,bkr->bhqk', qr_b, kr, preferred_element_type=jnp.float32)) * MLA_SCALE
            p = jax.nn.softmax(s, axis=-1).astype(v.dtype)
            return jnp.einsum('bhqk,bkhd->bqhd', p, v)
        o = sweep_query_blocks(blk, qn, qr)
        return o.reshape(o.shape[:2] + (MLA_HEADS * MLA_V,)) @ w_o

    a = h @ w_down
    qn, qr = parse_q(a[..., :MLA_Q_LORA], rope)
    kn, kr, v = parse_kv(a[..., MLA_Q_LORA:], rope)
    if need_ctx:
        ac = hc @ w_down
        qn_c, qr_c = parse_q(ac[..., :MLA_Q_LORA], None)
        kn_c, kr_c, v_c = parse_kv(ac[..., MLA_Q_LORA:], None)
    else:
        kn_c, kr_c, v_c = parse_kv(hc @ w_down[:, MLA_Q_LORA:], None)
    y = attend(qn, qr, jnp.concatenate([kn, kn_c], axis=1), jnp.concatenate([kr, kr_c], axis=1),
               jnp.concatenate([v, v_c], axis=1))
    yc = attend(qn_c, qr_c, kn_c, kr_c, v_c) if need_ctx else None
    return y, yc


def diff_mixer(h, hc, w_qkv, lam_vecs, g_subln, w_o, lam_init, rope, need_ctx):
    lv = lam_vecs.astype(jnp.float32)
    lam = jnp.exp(jnp.sum(lv[0] * lv[1])) - jnp.exp(jnp.sum(lv[2] * lv[3])) + lam_init

    def heads_qk(t):
        return t.reshape(t.shape[:2] + (DIFF_HEADS, 2, DIFF_HEAD_DIM))

    def heads_v(t):
        return t.reshape(t.shape[:2] + (DIFF_HEADS, 2 * DIFF_HEAD_DIM))

    def attend(q, k, v):
        def blk(q_b):
            s = jnp.einsum('bqhjd,bkhjd->bhjqk', q_b, k, preferred_element_type=jnp.float32) * DIFF_SCALE
            p = jax.nn.softmax(s, axis=-1)
            a = p[:, :, 0] - lam * p[:, :, 1]
            return jnp.einsum('bhqk,bkhd->bqhd', a.astype(v.dtype), v)
        o = sweep_query_blocks(blk, q)
        o = rms_norm(o, g_subln) * (1.0 - lam_init)
        return o.reshape(o.shape[:2] + (D_MODEL,)) @ w_o

    q, k, v = jnp.split(h @ w_qkv, 3, axis=-1)
    q = apply_axial_rope(heads_qk(q), rope)
    k = apply_axial_rope(heads_qk(k), rope)
    v = heads_v(v)
    if need_ctx:
        qc, kc, vc = jnp.split(hc @ w_qkv, 3, axis=-1)
        qc = heads_qk(qc)
    else:
        kc, vc = jnp.split(hc @ w_qkv[:, D_MODEL:], 2, axis=-1)
    kc, vc = heads_qk(kc), heads_v(vc)
    y = attend(q, jnp.concatenate([k, kc], axis=1), jnp.concatenate([v, vc], axis=1))
    yc = attend(qc, kc, vc) if need_ctx else None
    return y, yc


def sqrelu_mlp(h, w1, w2):
    return jnp.square(jax.nn.relu(h @ w1)) @ w2


def setup_inputs(seed: int = 0) -> dict:
    key = jax.random.key(seed)
    k = jax.random.split(key, 24)

    def nrm(i, shape, scale):
        return jax.random.normal(k[i], shape, jnp.float32) * scale

    n_pool = len(range(KIND_POOL, DEPTH, N_MIXERS))
    n_conv = len(range(KIND_CONV, DEPTH, N_MIXERS))
    n_mla = len(range(KIND_MLA, DEPTH, N_MIXERS))
    n_diff = len(range(KIND_DIFF, DEPTH, N_MIXERS))
    D = D_MODEL
    return dict(
        x=nrm(0, (BATCH, SEQ, D), 1.0),
        c=nrm(1, (BATCH, D), 1.0),
        ctx=nrm(2, (BATCH, CTX_LEN, D), 1.0),
        c_ctx=nrm(3, (D,), 1.0),
        ada_w=nrm(4, (DEPTH, D, 6 * D), 0.5 * D ** -0.5),
        ada_b=nrm(5, (DEPTH, 6 * D), 0.02),
        norm_g=1.0 + nrm(6, (DEPTH, 4, D), 0.05),
        ffn_w1=nrm(7, (DEPTH, D, D_FF), D ** -0.5),
        ffn_w2=nrm(8, (DEPTH, D_FF, D), D_FF ** -0.5),
        pool_w=nrm(9, (n_pool, len(POOL_WINDOWS), POOL_GROUP, POOL_GROUP), POOL_GROUP ** -0.5),
        pool_scale=1.0 + nrm(10, (n_pool, D), 0.1),
        conv_in_w=nrm(11, (n_conv, D, 3 * D), D ** -0.5),
        conv_w=nrm(12, (n_conv, CONV_WIDTH, D), CONV_WIDTH ** -0.5),
        conv_out_w=nrm(13, (n_conv, D, D), D ** -0.5),
        mla_w_down=nrm(14, (n_mla, D, MLA_Q_LORA + MLA_KV_LORA + MLA_ROPE), D ** -0.5),
        mla_g_q=1.0 + nrm(15, (n_mla, MLA_Q_LORA), 0.05),
        mla_g_kv=1.0 + nrm(16, (n_mla, MLA_KV_LORA), 0.05),
        mla_w_uq=nrm(17, (n_mla, MLA_Q_LORA, MLA_HEADS * (MLA_NOPE + MLA_ROPE)), MLA_Q_LORA ** -0.5),
        mla_w_ukv=nrm(18, (n_mla, MLA_KV_LORA, MLA_HEADS * (MLA_NOPE + MLA_V)), MLA_KV_LORA ** -0.5),
        mla_w_o=nrm(19, (n_mla, MLA_HEADS * MLA_V, D), (MLA_HEADS * MLA_V) ** -0.5),
        diff_w_qkv=nrm(20, (n_diff, D, 3 * D), D ** -0.5),
        diff_lambda=nrm(21, (n_diff, 4, DIFF_HEAD_DIM), 0.1),
        diff_g_subln=1.0 + nrm(22, (n_diff, 2 * DIFF_HEAD_DIM), 0.05),
        diff_w_o=nrm(23, (n_diff, D, D), D ** -0.5),
    )


def reference(x, c, ctx, c_ctx, ada_w, ada_b, norm_g, ffn_w1, ffn_w2, pool_w, pool_scale,
              conv_in_w, conv_w, conv_out_w, mla_w_down, mla_g_q, mla_g_kv, mla_w_uq, mla_w_ukv,
              mla_w_o, diff_w_qkv, diff_lambda, diff_g_subln, diff_w_o):
    rows = x.shape[1] // GRID_W
    rope_mla = axial_rope_tables(rows, MLA_ROPE)
    rope_diff = axial_rope_tables(rows, DIFF_HEAD_DIM)
    silu_c = jax.nn.silu(c)
    silu_cc = jax.nn.silu(c_ctx)
    xc = ctx
    for i in range(DEPTH):
        kind = i % N_MIXERS
        j = i // N_MIXERS
        ctx_out = i < DEPTH - 1
        ctx_in = ctx_out or kind in ATTN_KINDS
        sh_m, sc_m, gt_m, sh_f, sc_f, gt_f = jnp.split(
            (silu_c @ ada_w[i] + ada_b[i])[:, None, :], 6, axis=-1)
        h = modulate(rms_norm(x, norm_g[i, 0]), sh_m, sc_m)
        hc = None
        if ctx_in:
            mc = jnp.split(silu_cc @ ada_w[i] + ada_b[i], 6)
            hc = modulate(rms_norm(xc, norm_g[i, 0]), mc[0], mc[1])
        if kind == KIND_POOL:
            y = pool_mixer(h, pool_w[j], pool_scale[j])
            yc = pool_mixer(hc, pool_w[j], pool_scale[j]) if ctx_out else None
        elif kind == KIND_CONV:
            y = short_conv_mixer(h, conv_in_w[j], conv_w[j], conv_out_w[j])
            yc = short_conv_mixer(hc, conv_in_w[j], conv_w[j], conv_out_w[j]) if ctx_out else None
        elif kind == KIND_MLA:
            y, yc = mla_mixer(h, hc, mla_w_down[j], mla_g_q[j], mla_g_kv[j], mla_w_uq[j],
                              mla_w_ukv[j], mla_w_o[j], rope_mla, ctx_out)
        else:
            lam_init = 0.8 - 0.6 * math.exp(-0.3 * i)
            y, yc = diff_mixer(h, hc, diff_w_qkv[j], diff_lambda[j], diff_g_subln[j], diff_w_o[j],
                               lam_init, rope_diff, ctx_out)
        x = x + gt_m * rms_norm(y, norm_g[i, 1])
        hf = modulate(rms_norm(x, norm_g[i, 2]), sh_f, sc_f)
        x = x + gt_f * rms_norm(sqrelu_mlp(hf, ffn_w1[i], ffn_w2[i]), norm_g[i, 3])
        if ctx_out:
            xc = xc + mc[2] * rms_norm(yc, norm_g[i, 1])
            hcf = modulate(rms_norm(xc, norm_g[i, 2]), mc[3], mc[4])
            xc = xc + mc[5] * rms_norm(sqrelu_mlp(hcf, ffn_w1[i], ffn_w2[i]), norm_g[i, 3])
    return x
```

```python
import functools
import math

import jax
import jax.numpy as jnp
from jax import lax
from jax.experimental import pallas as pl
from jax.experimental.pallas import tpu as pltpu

F32 = jnp.float32
BF16 = jnp.bfloat16

GRID_W = 64
ROPE_BASE = 10000.0
NORM_EPS = 1e-6
POOL_WINDOWS = (2, 4, 8, 16)
CONV_WIDTH = 3
MLA_NOPE = 128
MLA_ROPE = 64
MLA_V = 128
DIFF_HEAD_DIM = 64
LANE = 128
HALO_F32 = 8
HALO_BF16 = 16
VMEM_LIMIT = 56 * 1024 * 1024
ROW_TILE_LATENT = 512
ROW_TILE_CONTEXT = 256
QUERY_TILE = 256


def _params(n_axes):
    return pltpu.CompilerParams(dimension_semantics=("arbitrary",) * n_axes,
                                vmem_limit_bytes=VMEM_LIMIT)


def _rms(x, g):
    return x * lax.rsqrt(jnp.mean(x * x, axis=-1, keepdims=True) + NORM_EPS) * g


def _dot(a, b):
    return jnp.dot(a, b, preferred_element_type=F32)


def _dot_nt(a, b):
    return lax.dot_general(a, b, (((1,), (1,)), ((), ())), preferred_element_type=F32)


def _ada_kernel(s_ref, w_ref, b_ref, o_ref):
    s = s_ref[...]
    a = (s / (1.0 + jnp.exp(-s))).astype(BF16)
    o_ref[...] = _dot(a, w_ref[...].astype(BF16)) + b_ref[...]


def _ada(s, ada_w, ada_b, tn=1536):
    depth, d, n = ada_w.shape
    rows = s.shape[0]
    return pl.pallas_call(
        _ada_kernel,
        grid=(depth, n // tn),
        in_specs=[pl.BlockSpec((rows, d), lambda l, j: (0, 0)),
                  pl.BlockSpec((None, d, tn), lambda l, j: (l, 0, j)),
                  pl.BlockSpec((None, 1, tn), lambda l, j: (l, 0, j))],
        out_specs=pl.BlockSpec((None, rows, tn), lambda l, j: (l, 0, j)),
        out_shape=jax.ShapeDtypeStruct((depth, rows, n), F32),
        compiler_params=_params(2),
        name="ada",
    )(s, ada_w, ada_b.reshape(depth, 1, n))


def _ffn_kernel(x_ref, mod_ref, g_ref, w1_ref, w2_ref, o_ref, hf_ref, acc_ref):
    k = pl.program_id(1)
    m = mod_ref[0]
    g = g_ref[...]

    @pl.when(k == 0)
    def _():
        hf_ref[...] = (_rms(x_ref[...], g[2:3]) * (1.0 + m[4:5]) + m[3:4]).astype(BF16)

    h1 = _dot(hf_ref[...], w1_ref[...])
    h1 = jnp.square(jnp.maximum(h1, 0.0)).astype(BF16)
    part = _dot(h1, w2_ref[...])

    @pl.when(k == 0)
    def _():
        acc_ref[...] = part

    @pl.when(k > 0)
    def _():
        acc_ref[...] += part

    @pl.when(k == pl.num_programs(1) - 1)
    def _():
        o_ref[...] = x_ref[...] + m[5:6] * _rms(acc_ref[...], g[3:4])


def _ffn(x, mod, norm_g, w1, w2, layer, tm=1024, fc=512):
    n, d = x.shape
    dff = w1.shape[-1]
    tm = min(tm, n)
    n_tiles = n // tm
    tpm = n_tiles // mod.shape[0]
    return pl.pallas_call(
        _ffn_kernel,
        grid=(n_tiles, dff // fc),
        in_specs=[pl.BlockSpec((tm, d), lambda i, k: (i, 0)),
                  pl.BlockSpec((1, 6, d), lambda i, k: (i // tpm, 0, 0)),
                  pl.BlockSpec((None, 4, d), lambda i, k: (layer, 0, 0)),
                  pl.BlockSpec((None, d, fc), lambda i, k: (layer, 0, k)),
                  pl.BlockSpec((None, fc, d), lambda i, k: (layer, k, 0))],
        out_specs=pl.BlockSpec((tm, d), lambda i, k: (i, 0)),
        out_shape=jax.ShapeDtypeStruct((n, d), F32),
        scratch_shapes=[pltpu.VMEM((tm, d), BF16), pltpu.VMEM((tm, d), F32)],
        compiler_params=_params(2),
        name="ffn",
    )(x, mod, norm_g, w1, w2)


def _halo_specs(t, halo, n, d):
    r = t // halo
    last = n // halo - 1
    return [pl.BlockSpec((t, d), lambda i: (i, 0)),
            pl.BlockSpec((halo, d), lambda i: (jnp.maximum(i * r - 1, 0), 0)),
            pl.BlockSpec((halo, d), lambda i: (jnp.minimum((i + 1) * r, last), 0))]


def _mod_spec(mod, tps, d):
    if mod.shape[0] == 1:
        return pl.BlockSpec((1, 6, d), lambda i: (0, 0, 0))
    return pl.BlockSpec((1, 6, d), lambda i: (i // tps, 0, 0))


def _pool_kernel(x_ref, xp_ref, xn_ref, mod_ref, g_ref, pw_ref, ps_ref, o_ref, hs_ref, *, t, seq, tps):
    i = pl.program_id(0)
    tile = lax.rem(i, tps)
    first = tile == 0
    last = tile == tps - 1
    m = mod_ref[0]
    g = g_ref[...]

    def hmod(xx):
        return _rms(xx, g[0:1]) * (1.0 + m[1:2]) + m[0:1]

    x = x_ref[...]
    h = hmod(x)
    hb = HALO_F32
    hs_ref[0:hb, :] = jnp.where(first, 0.0, hmod(xp_ref[...]))
    hs_ref[hb:hb + t, :] = h
    hs_ref[hb + t:2 * hb + t, :] = jnp.where(last, 0.0, hmod(xn_ref[...]))
    pos = tile * t + lax.broadcasted_iota(jnp.int32, (t, 1), 0)
    grp = pw_ref.shape[-1]
    ys = []
    for gi, w in enumerate(POOL_WINDOWS):
        lo = w // 2
        hi = w - 1 - lo
        cols = slice(gi * grp, (gi + 1) * grp)
        acc = hs_ref[hb - lo:hb - lo + t, cols]
        for o in range(-lo + 1, hi + 1):
            acc = acc + hs_ref[hb + o:hb + o + t, cols]
        cnt = (jnp.minimum(pos + hi + 1, seq) - jnp.maximum(pos - lo, 0)).astype(F32)
        dlt = acc * (1.0 / cnt) - h[:, cols]
        ys.append(_dot(dlt.astype(BF16), pw_ref[gi]))
    y = jnp.concatenate(ys, axis=-1) * ps_ref[...]
    o_ref[...] = x + m[2:3] * _rms(y, g[1:2])


def _pool(x, mod, norm_g, pool_w, pool_scale, layer, seq, t):
    n, d = x.shape
    tps = seq // t
    ng, grp, _ = pool_w.shape
    return pl.pallas_call(
        functools.partial(_pool_kernel, t=t, seq=seq, tps=tps),
        grid=(n // t,),
        in_specs=_halo_specs(t, HALO_F32, n, d) + [
            _mod_spec(mod, tps, d),
            pl.BlockSpec((None, 4, d), lambda i: (layer, 0, 0)),
            pl.BlockSpec((ng, grp, grp), lambda i: (0, 0, 0)),
            pl.BlockSpec((1, d), lambda i: (0, 0))],
        out_specs=pl.BlockSpec((t, d), lambda i: (i, 0)),
        out_shape=jax.ShapeDtypeStruct((n, d), F32),
        scratch_shapes=[pltpu.VMEM((t + 2 * HALO_F32, d), F32)],
        compiler_params=_params(1),
        name="pool",
    )(x, x, x, mod, norm_g, pool_w, pool_scale)


def _conv_kernel(x_ref, xp_ref, xn_ref, mod_ref, g_ref, win_ref, cw_ref, wout_ref, o_ref,
                 hs_ref, u_ref, *, t, tps):
    i = pl.program_id(0)
    tile = lax.rem(i, tps)
    first = tile == 0
    last = tile == tps - 1
    m = mod_ref[0]
    g = g_ref[...]
    d = x_ref.shape[-1]

    def hmod(xx):
        return _rms(xx, g[0:1]) * (1.0 + m[1:2]) + m[0:1]

    x = x_ref[...]
    hb = HALO_BF16
    hs_ref[0:hb, :] = jnp.where(first, 0.0, hmod(xp_ref[...])).astype(BF16)
    hs_ref[hb:hb + t, :] = hmod(x).astype(BF16)
    hs_ref[hb + t:2 * hb + t, :] = jnp.where(last, 0.0, hmod(xn_ref[...])).astype(BF16)
    hs = hs_ref[...]
    u_ref[...] = _dot(hs, win_ref[:, d:2 * d]) * _dot(hs, win_ref[:, 2 * d:3 * d])
    cw = cw_ref[...]
    z = (u_ref[hb - 1:hb - 1 + t, :] * cw[0:1] + u_ref[hb:hb + t, :] * cw[1:2]
         + u_ref[hb + 1:hb + 1 + t, :] * cw[2:3])
    b_gate = _dot(hs_ref[hb:hb + t, :], win_ref[:, 0:d])
    y = _dot((b_gate * z).astype(BF16), wout_ref[...])
    o_ref[...] = x + m[2:3] * _rms(y, g[1:2])


def _conv(x, mod, norm_g, w_in, conv_w, w_out, layer, seq, t):
    n, d = x.shape
    tps = seq // t
    return pl.pallas_call(
        functools.partial(_conv_kernel, t=t, tps=tps),
        grid=(n // t,),
        in_specs=_halo_specs(t, HALO_BF16, n, d) + [
            _mod_spec(mod, tps, d),
            pl.BlockSpec((None, 4, d), lambda i: (layer, 0, 0)),
            pl.BlockSpec((d, 3 * d), lambda i: (0, 0)),
            pl.BlockSpec((CONV_WIDTH, d), lambda i: (0, 0)),
            pl.BlockSpec((d, d), lambda i: (0, 0))],
        out_specs=pl.BlockSpec((t, d), lambda i: (i, 0)),
        out_shape=jax.ShapeDtypeStruct((n, d), F32),
        scratch_shapes=[pltpu.VMEM((t + 2 * HALO_BF16, d), BF16),
                        pltpu.VMEM((t + 2 * HALO_BF16, d), F32)],
        compiler_params=_params(1),
        name="conv",
    )(x, x, x, mod, norm_g, w_in, conv_w, w_out)


def _rope_table(seq, rot_dim):
    tpos = jnp.arange(seq)
    row = (tpos // GRID_W).astype(F32)
    col = (tpos % GRID_W).astype(F32)
    n = rot_dim // 2
    inv_freq = ROPE_BASE ** (-jnp.arange(0, n, 2, dtype=F32) / n)

    def ang(p):
        a = p[:, None] * inv_freq[None, :]
        return jnp.concatenate([a, a], axis=-1)

    a = jnp.concatenate([ang(row), ang(col)], axis=-1)
    a = jnp.tile(a, (1, LANE // rot_dim))
    return jnp.cos(a), jnp.sin(a)


def _identity_rope(seq):
    return jnp.ones((seq, LANE), F32), jnp.zeros((seq, LANE), F32)


def _rot_cols(w, rot_dim):
    half = rot_dim // 4
    shp = w.shape
    wg = w.reshape(shp[:-1] + (shp[-1] // (2 * half), 2, half))
    return jnp.stack([-wg[..., 1, :], wg[..., 0, :]], axis=-2).reshape(shp)


def _mla_proj_kernel(x_ref, mod_ref, g_ref, cos_ref, sin_ref, wd_ref, gq_ref, gkv_ref, wuq_ref,
                     wukv_ref, q_ref, k_ref, v_ref, *, heads, q_lora, kv_lora, scale):
    m = mod_ref[0]
    g = g_ref[...]
    h = (_rms(x_ref[...], g[0:1]) * (1.0 + m[1:2]) + m[0:1]).astype(BF16)
    a = _dot(h, wd_ref[...])
    cos = cos_ref[...]
    sin = sin_ref[...]
    c0 = q_lora + kv_lora
    kr = (a[:, c0:c0 + LANE] * cos + a[:, c0 + LANE:c0 + 2 * LANE] * sin).astype(BF16)
    cq = _rms(a[:, :q_lora], gq_ref[...]).astype(BF16)
    ckv = _rms(a[:, q_lora:c0], gkv_ref[...]).astype(BF16)
    qa = _dot(cq, wuq_ref[...])
    kv = _dot(ckv, wukv_ref[...])
    hw = heads * LANE
    for hh in range(heads):
        lo = hh * LANE
        qn = qa[:, lo:lo + LANE]
        qr = qa[:, hw + lo:hw + lo + LANE] * cos + qa[:, 2 * hw + lo:2 * hw + lo + LANE] * sin
        q_ref[:, 2 * lo:2 * lo + LANE] = (qn * scale).astype(BF16)
        q_ref[:, 2 * lo + LANE:2 * lo + 2 * LANE] = (qr * scale).astype(BF16)
        k_ref[:, 2 * lo:2 * lo + LANE] = kv[:, lo:lo + LANE].astype(BF16)
        k_ref[:, 2 * lo + LANE:2 * lo + 2 * LANE] = kr
    v_ref[...] = kv[:, hw:].astype(BF16)


def _mla_proj(x, mod, norm_g, cos, sin, wd, gq, gkv, wuq, wukv, layer, seq, t, heads, scale):
    n, d = x.shape
    tps = seq // t
    q_lora = gq.shape[-1]
    kv_lora = gkv.shape[-1]
    qw = heads * 2 * LANE
    return pl.pallas_call(
        functools.partial(_mla_proj_kernel, heads=heads, q_lora=q_lora, kv_lora=kv_lora, scale=scale),
        grid=(n // t,),
        in_specs=[pl.BlockSpec((t, d), lambda i: (i, 0)),
                  _mod_spec(mod, tps, d),
                  pl.BlockSpec((None, 4, d), lambda i: (layer, 0, 0)),
                  pl.BlockSpec((t, LANE), lambda i: (i % tps, 0)),
                  pl.BlockSpec((t, LANE), lambda i: (i % tps, 0)),
                  pl.BlockSpec(wd.shape, lambda i: (0, 0)),
                  pl.BlockSpec((1, q_lora), lambda i: (0, 0)),
                  pl.BlockSpec((1, kv_lora), lambda i: (0, 0)),
                  pl.BlockSpec(wuq.shape, lambda i: (0, 0)),
                  pl.BlockSpec(wukv.shape, lambda i: (0, 0))],
        out_specs=[pl.BlockSpec((t, qw), lambda i: (i, 0)),
                   pl.BlockSpec((t, qw), lambda i: (i, 0)),
                   pl.BlockSpec((t, heads * LANE), lambda i: (i, 0))],
        out_shape=[jax.ShapeDtypeStruct((n, qw), BF16),
                   jax.ShapeDtypeStruct((n, qw), BF16),
                   jax.ShapeDtypeStruct((n, heads * LANE), BF16)],
        compiler_params=_params(1),
        name="mla_proj",
    )(x, mod, norm_g, cos, sin, wd, gq, gkv, wuq, wukv)


def _diff_proj_kernel(x_ref, mod_ref, g_ref, cos_ref, sin_ref, w_ref, *out_refs, need_q, scale):
    m = mod_ref[0]
    g = g_ref[...]
    d = x_ref.shape[-1]
    h = (_rms(x_ref[...], g[0:1]) * (1.0 + m[1:2]) + m[0:1]).astype(BF16)
    cos = cos_ref[...]
    sin = sin_ref[...]

    def roped(col0, ref, mul):
        y = _dot(h, w_ref[:, col0:col0 + 2 * d])
        for j in range(d // LANE):
            lo = j * LANE
            r = y[:, lo:lo + LANE] * cos + y[:, d + lo:d + lo + LANE] * sin
            ref[:, lo:lo + LANE] = (r * mul).astype(BF16) if mul != 1.0 else r.astype(BF16)

    if need_q:
        q_ref, k_ref, v_ref = out_refs
        roped(0, q_ref, scale)
    else:
        k_ref, v_ref = out_refs
    roped(2 * d, k_ref, 1.0)
    v_ref[...] = _dot(h, w_ref[:, 4 * d:5 * d]).astype(BF16)


def _diff_proj(x, mod, norm_g, cos, sin, w_ext, layer, seq, t, need_q, scale):
    n, d = x.shape
    tps = seq // t
    n_out = 3 if need_q else 2
    return pl.pallas_call(
        functools.partial(_diff_proj_kernel, need_q=need_q, scale=scale),
        grid=(n // t,),
        in_specs=[pl.BlockSpec((t, d), lambda i: (i, 0)),
                  _mod_spec(mod, tps, d),
                  pl.BlockSpec((None, 4, d), lambda i: (layer, 0, 0)),
                  pl.BlockSpec((t, LANE), lambda i: (i % tps, 0)),
                  pl.BlockSpec((t, LANE), lambda i: (i % tps, 0)),
                  pl.BlockSpec(w_ext.shape, lambda i: (0, 0))],
        out_specs=[pl.BlockSpec((t, d), lambda i: (i, 0))] * n_out,
        out_shape=[jax.ShapeDtypeStruct((n, d), BF16)] * n_out,
        compiler_params=_params(1),
        name="diff_proj",
    )(x, mod, norm_g, cos, sin, w_ext)


def _attn_kernel(*refs, kind, n_pieces, heads, lam_init):
    x_ref, mod_ref, g_ref, q_ref = refs[:4]
    kv_refs = refs[4:4 + 2 * n_pieces]
    k_refs = kv_refs[0::2]
    v_refs = kv_refs[1::2]
    rest = refs[4 + 2 * n_pieces:]
    if kind == "diff":
        lam_ref, gs_ref, wo_ref, o_ref, oh_ref = rest
        lv = lam_ref[...]
        lam = (jnp.exp(jnp.sum(lv[0:1] * lv[1:2], axis=-1, keepdims=True))
               - jnp.exp(jnp.sum(lv[2:3] * lv[3:4], axis=-1, keepdims=True)) + lam_init)
    else:
        wo_ref, o_ref, oh_ref = rest
    m = mod_ref[0]
    g = g_ref[...]
    tq = q_ref.shape[0]

    def softmax_parts(qh, kcols):
        s = [_dot_nt(qh, k_ref[:, kcols]) for k_ref in k_refs]
        mx = functools.reduce(jnp.maximum, [jnp.max(si, axis=-1, keepdims=True) for si in s])
        p = [jnp.exp(si - mx) for si in s]
        l = functools.reduce(jnp.add, [jnp.sum(pi, axis=-1, keepdims=True) for pi in p])
        return p, l

    for hh in range(heads):
        vcols = slice(hh * LANE, (hh + 1) * LANE)
        if kind == "mla":
            kcols = slice(hh * 2 * LANE, (hh + 1) * 2 * LANE)
            p, l = softmax_parts(q_ref[:, kcols], kcols)
            o = functools.reduce(jnp.add, [_dot(pi.astype(BF16), v_ref[:, vcols])
                                           for pi, v_ref in zip(p, v_refs)])
            o = o * (1.0 / l)
        else:
            qh = q_ref[:, vcols]
            lane = lax.broadcasted_iota(jnp.int32, (tq, LANE), 1)
            zero = jnp.zeros_like(qh)
            p0, l0 = softmax_parts(jnp.where(lane < DIFF_HEAD_DIM, qh, zero), vcols)
            p1, l1 = softmax_parts(jnp.where(lane >= DIFF_HEAD_DIM, qh, zero), vcols)
            c0 = 1.0 / l0
            c1 = lam / l1
            o = functools.reduce(jnp.add, [_dot((a0 * c0 - a1 * c1).astype(BF16), v_ref[:, vcols])
                                           for a0, a1, v_ref in zip(p0, p1, v_refs)])
            o = _rms(o, gs_ref[...]) * (1.0 - lam_init)
        oh_ref[:, vcols] = o.astype(BF16)
    y = _dot(oh_ref[...], wo_ref[...])
    o_ref[...] = x_ref[...] + m[2:3] * _rms(y, g[1:2])


def _attn(x, mod, norm_g, q, pieces, wo, layer, seq, tq, kind, heads, extra=(), lam_init=0.0):
    n, d = x.shape
    tps = seq // tq
    qw = q.shape[-1]
    in_specs = [pl.BlockSpec((tq, d), lambda b, i: (b * tps + i, 0)),
                (pl.BlockSpec((1, 6, d), lambda b, i: (0, 0, 0)) if mod.shape[0] == 1
                 else pl.BlockSpec((1, 6, d), lambda b, i: (b, 0, 0))),
                pl.BlockSpec((None, 4, d), lambda b, i: (layer, 0, 0)),
                pl.BlockSpec((tq, qw), lambda b, i: (b * tps + i, 0))]
    args = [x, mod, norm_g, q]
    n_seq = n // seq
    for kp, vp in pieces:
        lk = kp.shape[0] // n_seq
        in_specs += [pl.BlockSpec((lk, kp.shape[-1]), lambda b, i: (b, 0)),
                     pl.BlockSpec((lk, vp.shape[-1]), lambda b, i: (b, 0))]
        args += [kp, vp]
    for e in extra:
        in_specs.append(pl.BlockSpec(e.shape, lambda b, i: (0, 0)))
        args.append(e)
    in_specs.append(pl.BlockSpec(wo.shape, lambda b, i: (0, 0)))
    args.append(wo)
    return pl.pallas_call(
        functools.partial(_attn_kernel, kind=kind, n_pieces=len(pieces), heads=heads, lam_init=lam_init),
        grid=(n_seq, tps),
        in_specs=in_specs,
        out_specs=pl.BlockSpec((tq, d), lambda b, i: (b * tps + i, 0)),
        out_shape=jax.ShapeDtypeStruct((n, d), F32),
        scratch_shapes=[pltpu.VMEM((tq, d), BF16)],
        compiler_params=_params(2),
        name="attn_" + kind,
    )(*args)


def kernel(x, c, ctx, c_ctx, ada_w, ada_b, norm_g, ffn_w1, ffn_w2, pool_w, pool_scale, conv_in_w, conv_w,
           conv_out_w, mla_w_down, mla_g_q, mla_g_kv, mla_w_uq, mla_w_ukv, mla_w_o, diff_w_qkv,
           diff_lambda, diff_g_subln, diff_w_o):
    bsz, seq, d = x.shape
    lc = ctx.shape[1]
    depth = ada_w.shape[0]
    n_mixers = 4

    rows = bsz + 1
    rows_pad = -(-rows // 32) * 32
    s = jnp.concatenate([c, c_ctx[None, :], jnp.zeros((rows_pad - rows, d), F32)], axis=0)
    mod_all = _ada(s, ada_w, ada_b).reshape(depth, rows_pad, 6, d)

    w1 = ffn_w1.astype(BF16)
    w2 = ffn_w2.astype(BF16)
    xl = x.reshape(bsz * seq, d)
    xc = ctx.reshape(bsz * lc, d)
    t_lat = min(ROW_TILE_LATENT, seq)
    t_ctx = min(ROW_TILE_CONTEXT, lc)
    tq = min(QUERY_TILE, seq)

    for i in range(depth):
        kind = i % n_mixers
        j = i // n_mixers
        ctx_out = i < depth - 1
        mod_l = mod_all[i, :bsz]
        mod_c = mod_all[i, bsz:bsz + 1]
        if kind == 0:
            pw = pool_w[j].astype(BF16)
            ps = pool_scale[j][None, :]
            xl = _pool(xl, mod_l, norm_g, pw, ps, i, seq, t_lat)
            if ctx_out:
                xc = _pool(xc, mod_c, norm_g, pw, ps, i, lc, t_ctx)
        elif kind == 1:
            w_in = conv_in_w[j].astype(BF16)
            w_out = conv_out_w[j].astype(BF16)
            xl = _conv(xl, mod_l, norm_g, w_in, conv_w[j], w_out, i, seq, t_lat)
            if ctx_out:
                xc = _conv(xc, mod_c, norm_g, w_in, conv_w[j], w_out, i, lc, t_ctx)
        elif kind == 2:
            heads = mla_w_uq.shape[-1] // (MLA_NOPE + MLA_ROPE)
            q_lora = mla_g_q.shape[-1]
            kv_lora = mla_g_kv.shape[-1]
            scale = (MLA_NOPE + MLA_ROPE) ** -0.5
            wdn = mla_w_down[j]
            wkr = wdn[:, q_lora + kv_lora:]
            zpad = jnp.zeros((d, LANE - MLA_ROPE), F32)
            wd = jnp.concatenate([wdn[:, :q_lora + kv_lora], wkr, zpad, _rot_cols(wkr, MLA_ROPE), zpad],
                                 axis=-1).astype(BF16)
            wq = mla_w_uq[j].reshape(q_lora, heads, MLA_NOPE + MLA_ROPE)
            wq_r = wq[:, :, MLA_NOPE:]
            zq = jnp.zeros((q_lora, heads, LANE - MLA_ROPE), F32)
            wuq = jnp.concatenate([
                wq[:, :, :MLA_NOPE].reshape(q_lora, heads * MLA_NOPE),
                jnp.concatenate([wq_r, zq], axis=-1).reshape(q_lora, heads * LANE),
                jnp.concatenate([_rot_cols(wq_r, MLA_ROPE), zq], axis=-1).reshape(q_lora, heads * LANE)],
                axis=-1).astype(BF16)
            wkv = mla_w_ukv[j].reshape(kv_lora, heads, MLA_NOPE + MLA_V)
            wukv = jnp.concatenate([wkv[:, :, :MLA_NOPE].reshape(kv_lora, heads * MLA_NOPE),
                                    wkv[:, :, MLA_NOPE:].reshape(kv_lora, heads * MLA_V)],
                                   axis=-1).astype(BF16)
            wo = mla_w_o[j].astype(BF16)
            gq = mla_g_q[j][None, :]
            gkv = mla_g_kv[j][None, :]
            cos_l, sin_l = _rope_table(seq, MLA_ROPE)
            cos_c, sin_c = _identity_rope(lc)
            ql, kl, vl = _mla_proj(xl, mod_l, norm_g, cos_l, sin_l, wd, gq, gkv, wuq, wukv, i, seq, t_lat,
                                   heads, scale)
            qc, kc, vc = _mla_proj(xc, mod_c, norm_g, cos_c, sin_c, wd, gq, gkv, wuq, wukv, i, lc, t_ctx,
                                   heads, scale)
            xl = _attn(xl, mod_l, norm_g, ql, [(kl, vl), (kc, vc)], wo, i, seq, tq, "mla", heads)
            if ctx_out:
                xc = _attn(xc, mod_c, norm_g, qc, [(kc, vc)], wo, i, lc, t_ctx, "mla", heads)
        else:
            heads = d // (2 * DIFF_HEAD_DIM)
            lam_init = 0.8 - 0.6 * math.exp(-0.3 * i)
            scale = DIFF_HEAD_DIM ** -0.5
            wqkv = diff_w_qkv[j]
            wq, wk, wv = wqkv[:, :d], wqkv[:, d:2 * d], wqkv[:, 2 * d:]
            w_ext = jnp.concatenate([wq, _rot_cols(wq, DIFF_HEAD_DIM), wk, _rot_cols(wk, DIFF_HEAD_DIM), wv],
                                    axis=-1).astype(BF16)
            wo = diff_w_o[j].astype(BF16)
            cos_l, sin_l = _rope_table(seq, DIFF_HEAD_DIM)
            cos_c, sin_c = _identity_rope(lc)
            ql, kl, vl = _diff_proj(xl, mod_l, norm_g, cos_l, sin_l, w_ext, i, seq, t_lat, True, scale)
            outs_c = _diff_proj(xc, mod_c, norm_g, cos_c, sin_c, w_ext, i, lc, t_ctx, ctx_out, scale)
            kc, vc = outs_c[-2], outs_c[-1]
            extra = (diff_lambda[j], diff_g_subln[j][None, :])
            xl = _attn(xl, mod_l, norm_g, ql, [(kl, vl), (kc, vc)], wo, i, seq, tq, "diff", heads,
                       extra, lam_init)
            if ctx_out:
                xc = _attn(xc, mod_c, norm_g, outs_c[0], [(kc, vc)], wo, i, lc, t_ctx, "diff", heads,
                           extra, lam_init)
        xl = _ffn(xl, mod_l, norm_g, w1, w2, i)
        if ctx_out:
            xc = _ffn(xc, mod_c, norm_g, w1, w2, i)
    return xl.reshape(bsz, seq, d)
```

```python
import functools
import math

import jax
import jax.numpy as jnp
from jax import lax
from jax.experimental import pallas as pl
from jax.experimental.pallas import tpu as pltpu

F32 = jnp.float32
BF16 = jnp.bfloat16

GRID_W = 64
ROPE_BASE = 10000.0
NORM_EPS = 1e-6
LOG2E = math.log2(math.e)
POOL_WINDOWS = (2, 4, 8, 16)
CONV_WIDTH = 3
MLA_NOPE = 128
MLA_ROPE = 64
MLA_V = 128
DIFF_HEAD_DIM = 64
LANE = 128
HALO_F32 = 8
HALO_BF16 = 16
VMEM_LIMIT = 56 * 1024 * 1024
ROW_TILE_LATENT = 512
ROW_TILE_CONTEXT = 256
QUERY_TILE = 256
FFN_ROW_TILE = 512
FFN_HIDDEN_CHUNK = 512


def _params(n_axes):
    return pltpu.CompilerParams(dimension_semantics=("arbitrary",) * n_axes,
                                vmem_limit_bytes=VMEM_LIMIT)


def _rms(x, g):
    return x * lax.rsqrt(jnp.mean(x * x, axis=-1, keepdims=True) + NORM_EPS) * g


def _dot(a, b):
    return jnp.dot(a, b, preferred_element_type=F32)


def _dot_nt(a, b):
    return lax.dot_general(a, b, (((1,), (1,)), ((), ())), preferred_element_type=F32)


def _ada_kernel(s_ref, w_ref, b_ref, o_ref):
    s = s_ref[...]
    a = (s / (1.0 + jnp.exp(-s))).astype(BF16)
    o_ref[...] = _dot(a, w_ref[...].astype(BF16)) + b_ref[...]


def _ada(s, ada_w, ada_b, tn=1536):
    depth, d, n = ada_w.shape
    rows = s.shape[0]
    return pl.pallas_call(
        _ada_kernel,
        grid=(depth, n // tn),
        in_specs=[pl.BlockSpec((rows, d), lambda l, j: (0, 0)),
                  pl.BlockSpec((None, d, tn), lambda l, j: (l, 0, j)),
                  pl.BlockSpec((None, 1, tn), lambda l, j: (l, 0, j))],
        out_specs=pl.BlockSpec((None, rows, tn), lambda l, j: (l, 0, j)),
        out_shape=jax.ShapeDtypeStruct((depth, rows, n), F32),
        compiler_params=_params(2),
        name="ada",
    )(s, ada_w, ada_b.reshape(depth, 1, n))


def _ffn_kernel(x_ref, mod_ref, g_ref, w1_ref, w2_ref, o_ref, h1_ref, *, fc):
    m = mod_ref[0]
    g = g_ref[...]
    x = x_ref[...]
    hf = (_rms(x, g[2:3]) * (1.0 + m[4:5]) + m[3:4]).astype(BF16)
    for c in range(h1_ref.shape[-1] // fc):
        cols = slice(c * fc, (c + 1) * fc)
        h1_ref[:, cols] = jnp.square(jnp.maximum(_dot(hf, w1_ref[:, cols]), 0.0)).astype(BF16)
    y = _dot(h1_ref[...], w2_ref[...])
    o_ref[...] = x + m[5:6] * _rms(y, g[3:4])


def _resident(block_shape, index_map):
    return pl.BlockSpec(block_shape, index_map, pipeline_mode=pl.Buffered(1))


def _ffn(x, mod, norm_g, w1, w2, layer, tm=FFN_ROW_TILE, fc=FFN_HIDDEN_CHUNK):
    n, d = x.shape
    dff = w1.shape[-1]
    tm = min(tm, n)
    n_tiles = n // tm
    tpm = n_tiles // mod.shape[0]
    return pl.pallas_call(
        functools.partial(_ffn_kernel, fc=fc),
        grid=(n_tiles,),
        in_specs=[pl.BlockSpec((tm, d), lambda i: (i, 0)),
                  pl.BlockSpec((1, 6, d), lambda i: (i // tpm, 0, 0)),
                  pl.BlockSpec((None, 4, d), lambda i: (layer, 0, 0)),
                  _resident((None, d, dff), lambda i: (layer, 0, 0)),
                  _resident((None, dff, d), lambda i: (layer, 0, 0))],
        out_specs=pl.BlockSpec((tm, d), lambda i: (i, 0)),
        out_shape=jax.ShapeDtypeStruct((n, d), F32),
        scratch_shapes=[pltpu.VMEM((tm, dff), BF16)],
        compiler_params=_params(1),
        name="ffn",
    )(x, mod, norm_g, w1, w2)


def _halo_specs(t, halo, n, d):
    r = t // halo
    last = n // halo - 1
    return [pl.BlockSpec((t, d), lambda i: (i, 0)),
            pl.BlockSpec((halo, d), lambda i: (jnp.maximum(i * r - 1, 0), 0)),
            pl.BlockSpec((halo, d), lambda i: (jnp.minimum((i + 1) * r, last), 0))]


def _mod_spec(mod, tps, d):
    if mod.shape[0] == 1:
        return pl.BlockSpec((1, 6, d), lambda i: (0, 0, 0))
    return pl.BlockSpec((1, 6, d), lambda i: (i // tps, 0, 0))


def _pool_kernel(x_ref, xp_ref, xn_ref, mod_ref, g_ref, pw_ref, ps_ref, o_ref, hs_ref, *, t, seq, tps):
    i = pl.program_id(0)
    tile = lax.rem(i, tps)
    first = tile == 0
    last = tile == tps - 1
    m = mod_ref[0]
    g = g_ref[...]

    def hmod(xx):
        return _rms(xx, g[0:1]) * (1.0 + m[1:2]) + m[0:1]

    x = x_ref[...]
    h = hmod(x)
    hb = HALO_F32
    hs_ref[0:hb, :] = jnp.where(first, 0.0, hmod(xp_ref[...]))
    hs_ref[hb:hb + t, :] = h
    hs_ref[hb + t:2 * hb + t, :] = jnp.where(last, 0.0, hmod(xn_ref[...]))
    pos = tile * t + lax.broadcasted_iota(jnp.int32, (t, 1), 0)
    grp = pw_ref.shape[-1]
    ys = []
    for gi, w in enumerate(POOL_WINDOWS):
        lo = w // 2
        hi = w - 1 - lo
        cols = slice(gi * grp, (gi + 1) * grp)
        acc = hs_ref[hb - lo:hb - lo + t, cols]
        for o in range(-lo + 1, hi + 1):
            acc = acc + hs_ref[hb + o:hb + o + t, cols]
        cnt = (jnp.minimum(pos + hi + 1, seq) - jnp.maximum(pos - lo, 0)).astype(F32)
        dlt = acc * (1.0 / cnt) - h[:, cols]
        ys.append(_dot(dlt.astype(BF16), pw_ref[gi]))
    y = jnp.concatenate(ys, axis=-1) * ps_ref[...]
    o_ref[...] = x + m[2:3] * _rms(y, g[1:2])


def _pool(x, mod, norm_g, pool_w, pool_scale, layer, seq, t):
    n, d = x.shape
    tps = seq // t
    ng, grp, _ = pool_w.shape
    return pl.pallas_call(
        functools.partial(_pool_kernel, t=t, seq=seq, tps=tps),
        grid=(n // t,),
        in_specs=_halo_specs(t, HALO_F32, n, d) + [
            _mod_spec(mod, tps, d),
            pl.BlockSpec((None, 4, d), lambda i: (layer, 0, 0)),
            pl.BlockSpec((ng, grp, grp), lambda i: (0, 0, 0)),
            pl.BlockSpec((1, d), lambda i: (0, 0))],
        out_specs=pl.BlockSpec((t, d), lambda i: (i, 0)),
        out_shape=jax.ShapeDtypeStruct((n, d), F32),
        scratch_shapes=[pltpu.VMEM((t + 2 * HALO_F32, d), F32)],
        compiler_params=_params(1),
        name="pool",
    )(x, x, x, mod, norm_g, pool_w, pool_scale)


def _conv_kernel(x_ref, xp_ref, xn_ref, mod_ref, g_ref, win_ref, cw_ref, wout_ref, o_ref,
                 hs_ref, u_ref, *, t, tps):
    i = pl.program_id(0)
    tile = lax.rem(i, tps)
    first = tile == 0
    last = tile == tps - 1
    m = mod_ref[0]
    g = g_ref[...]
    d = x_ref.shape[-1]

    def hmod(xx):
        return _rms(xx, g[0:1]) * (1.0 + m[1:2]) + m[0:1]

    x = x_ref[...]
    hb = HALO_BF16
    hs_ref[0:hb, :] = jnp.where(first, 0.0, hmod(xp_ref[...])).astype(BF16)
    hs_ref[hb:hb + t, :] = hmod(x).astype(BF16)
    hs_ref[hb + t:2 * hb + t, :] = jnp.where(last, 0.0, hmod(xn_ref[...])).astype(BF16)
    hs = hs_ref[...]
    u_ref[...] = _dot(hs, win_ref[:, d:2 * d]) * _dot(hs, win_ref[:, 2 * d:3 * d])
    cw = cw_ref[...]
    z = (u_ref[hb - 1:hb - 1 + t, :] * cw[0:1] + u_ref[hb:hb + t, :] * cw[1:2]
         + u_ref[hb + 1:hb + 1 + t, :] * cw[2:3])
    b_gate = _dot(hs_ref[hb:hb + t, :], win_ref[:, 0:d])
    y = _dot((b_gate * z).astype(BF16), wout_ref[...])
    o_ref[...] = x + m[2:3] * _rms(y, g[1:2])


def _conv(x, mod, norm_g, w_in, conv_w, w_out, layer, seq, t):
    n, d = x.shape
    tps = seq // t
    return pl.pallas_call(
        functools.partial(_conv_kernel, t=t, tps=tps),
        grid=(n // t,),
        in_specs=_halo_specs(t, HALO_BF16, n, d) + [
            _mod_spec(mod, tps, d),
            pl.BlockSpec((None, 4, d), lambda i: (layer, 0, 0)),
            pl.BlockSpec((d, 3 * d), lambda i: (0, 0)),
            pl.BlockSpec((CONV_WIDTH, d), lambda i: (0, 0)),
            pl.BlockSpec((d, d), lambda i: (0, 0))],
        out_specs=pl.BlockSpec((t, d), lambda i: (i, 0)),
        out_shape=jax.ShapeDtypeStruct((n, d), F32),
        scratch_shapes=[pltpu.VMEM((t + 2 * HALO_BF16, d), BF16),
                        pltpu.VMEM((t + 2 * HALO_BF16, d), F32)],
        compiler_params=_params(1),
        name="conv",
    )(x, x, x, mod, norm_g, w_in, conv_w, w_out)


def _rope_table(seq, rot_dim):
    tpos = jnp.arange(seq)
    row = (tpos // GRID_W).astype(F32)
    col = (tpos % GRID_W).astype(F32)
    n = rot_dim // 2
    inv_freq = ROPE_BASE ** (-jnp.arange(0, n, 2, dtype=F32) / n)

    def ang(p):
        a = p[:, None] * inv_freq[None, :]
        return jnp.concatenate([a, a], axis=-1)

    a = jnp.concatenate([ang(row), ang(col)], axis=-1)
    a = jnp.tile(a, (1, LANE // rot_dim))
    return jnp.cos(a), jnp.sin(a)


def _identity_rope(seq):
    return jnp.ones((seq, LANE), F32), jnp.zeros((seq, LANE), F32)


def _rot_cols(w, rot_dim):
    half = rot_dim // 4
    shp = w.shape
    wg = w.reshape(shp[:-1] + (shp[-1] // (2 * half), 2, half))
    return jnp.stack([-wg[..., 1, :], wg[..., 0, :]], axis=-2).reshape(shp)


def _mla_proj_kernel(x_ref, mod_ref, g_ref, cos_ref, sin_ref, wd_ref, gq_ref, gkv_ref, wuq_ref,
                     wukv_ref, q_ref, k_ref, v_ref, *, heads, q_lora, kv_lora, scale):
    m = mod_ref[0]
    g = g_ref[...]
    h = (_rms(x_ref[...], g[0:1]) * (1.0 + m[1:2]) + m[0:1]).astype(BF16)
    a = _dot(h, wd_ref[...])
    cos = cos_ref[...]
    sin = sin_ref[...]
    c0 = q_lora + kv_lora
    kr = (a[:, c0:c0 + LANE] * cos + a[:, c0 + LANE:c0 + 2 * LANE] * sin).astype(BF16)
    cq = _rms(a[:, :q_lora], gq_ref[...]).astype(BF16)
    ckv = _rms(a[:, q_lora:c0], gkv_ref[...]).astype(BF16)
    qa = _dot(cq, wuq_ref[...])
    kv = _dot(ckv, wukv_ref[...])
    hw = heads * LANE
    for hh in range(heads):
        lo = hh * LANE
        qn = qa[:, lo:lo + LANE]
        qr = qa[:, hw + lo:hw + lo + LANE] * cos + qa[:, 2 * hw + lo:2 * hw + lo + LANE] * sin
        q_ref[:, 2 * lo:2 * lo + LANE] = (qn * scale).astype(BF16)
        q_ref[:, 2 * lo + LANE:2 * lo + 2 * LANE] = (qr * scale).astype(BF16)
        k_ref[:, 2 * lo:2 * lo + LANE] = kv[:, lo:lo + LANE].astype(BF16)
        k_ref[:, 2 * lo + LANE:2 * lo + 2 * LANE] = kr
    v_ref[...] = kv[:, hw:].astype(BF16)


def _mla_proj(x, mod, norm_g, cos, sin, wd, gq, gkv, wuq, wukv, layer, seq, t, heads, scale):
    n, d = x.shape
    tps = seq // t
    q_lora = gq.shape[-1]
    kv_lora = gkv.shape[-1]
    qw = heads * 2 * LANE
    return pl.pallas_call(
        functools.partial(_mla_proj_kernel, heads=heads, q_lora=q_lora, kv_lora=kv_lora, scale=scale),
        grid=(n // t,),
        in_specs=[pl.BlockSpec((t, d), lambda i: (i, 0)),
                  _mod_spec(mod, tps, d),
                  pl.BlockSpec((None, 4, d), lambda i: (layer, 0, 0)),
                  pl.BlockSpec((t, LANE), lambda i: (i % tps, 0)),
                  pl.BlockSpec((t, LANE), lambda i: (i % tps, 0)),
                  pl.BlockSpec(wd.shape, lambda i: (0, 0)),
                  pl.BlockSpec((1, q_lora), lambda i: (0, 0)),
                  pl.BlockSpec((1, kv_lora), lambda i: (0, 0)),
                  pl.BlockSpec(wuq.shape, lambda i: (0, 0)),
                  pl.BlockSpec(wukv.shape, lambda i: (0, 0))],
        out_specs=[pl.BlockSpec((t, qw), lambda i: (i, 0)),
                   pl.BlockSpec((t, qw), lambda i: (i, 0)),
                   pl.BlockSpec((t, heads * LANE), lambda i: (i, 0))],
        out_shape=[jax.ShapeDtypeStruct((n, qw), BF16),
                   jax.ShapeDtypeStruct((n, qw), BF16),
                   jax.ShapeDtypeStruct((n, heads * LANE), BF16)],
        compiler_params=_params(1),
        name="mla_proj",
    )(x, mod, norm_g, cos, sin, wd, gq, gkv, wuq, wukv)


def _diff_proj_kernel(x_ref, mod_ref, g_ref, cos_ref, sin_ref, w_ref, *out_refs, need_q, scale):
    m = mod_ref[0]
    g = g_ref[...]
    d = x_ref.shape[-1]
    h = (_rms(x_ref[...], g[0:1]) * (1.0 + m[1:2]) + m[0:1]).astype(BF16)
    cos = cos_ref[...]
    sin = sin_ref[...]

    def roped(col0, ref, mul):
        y = _dot(h, w_ref[:, col0:col0 + 2 * d])
        for j in range(d // LANE):
            lo = j * LANE
            r = y[:, lo:lo + LANE] * cos + y[:, d + lo:d + lo + LANE] * sin
            ref[:, lo:lo + LANE] = (r * mul).astype(BF16) if mul != 1.0 else r.astype(BF16)

    if need_q:
        q_ref, k_ref, v_ref = out_refs
        roped(0, q_ref, scale)
    else:
        k_ref, v_ref = out_refs
    roped(2 * d, k_ref, 1.0)
    v_ref[...] = _dot(h, w_ref[:, 4 * d:5 * d]).astype(BF16)


def _diff_proj(x, mod, norm_g, cos, sin, w_ext, layer, seq, t, need_q, scale):
    n, d = x.shape
    tps = seq // t
    n_out = 3 if need_q else 2
    return pl.pallas_call(
        functools.partial(_diff_proj_kernel, need_q=need_q, scale=scale),
        grid=(n // t,),
        in_specs=[pl.BlockSpec((t, d), lambda i: (i, 0)),
                  _mod_spec(mod, tps, d),
                  pl.BlockSpec((None, 4, d), lambda i: (layer, 0, 0)),
                  pl.BlockSpec((t, LANE), lambda i: (i % tps, 0)),
                  pl.BlockSpec((t, LANE), lambda i: (i % tps, 0)),
                  pl.BlockSpec(w_ext.shape, lambda i: (0, 0))],
        out_specs=[pl.BlockSpec((t, d), lambda i: (i, 0))] * n_out,
        out_shape=[jax.ShapeDtypeStruct((n, d), BF16)] * n_out,
        compiler_params=_params(1),
        name="diff_proj",
    )(x, mod, norm_g, cos, sin, w_ext)


def _attn_kernel(*refs, kind, n_pieces, heads, lam_init):
    x_ref, mod_ref, g_ref, q_ref = refs[:4]
    kv_refs = refs[4:4 + 2 * n_pieces]
    k_refs = kv_refs[0::2]
    v_refs = kv_refs[1::2]
    rest = refs[4 + 2 * n_pieces:]
    if kind == "diff":
        lam_ref, gs_ref, wo_ref, o_ref, oh_ref = rest[:5]
        lv = lam_ref[...]
        lam = (jnp.exp(jnp.sum(lv[0:1] * lv[1:2], axis=-1, keepdims=True))
               - jnp.exp(jnp.sum(lv[2:3] * lv[3:4], axis=-1, keepdims=True)) + lam_init)
    else:
        wo_ref, o_ref, oh_ref = rest[:3]
    ve_refs = rest[-n_pieces:]
    m = mod_ref[0]
    g = g_ref[...]
    tq = q_ref.shape[0]

    @pl.when(pl.program_id(1) == 0)
    def _():
        for v_ref, ve_ref in zip(v_refs, ve_refs):
            ones = jnp.ones((v_ref.shape[0], LANE), BF16)
            for hh in range(heads):
                ve_ref[:, 2 * hh * LANE:(2 * hh + 1) * LANE] = v_ref[:, hh * LANE:(hh + 1) * LANE]
                ve_ref[:, (2 * hh + 1) * LANE:(2 * hh + 2) * LANE] = ones

    def attend(qs, kcols, hh):
        s = [_dot_nt(qs, k_ref[:, kcols]) for k_ref in k_refs]
        mx = functools.reduce(jnp.maximum, [jnp.max(si, axis=-1, keepdims=True) for si in s])
        vcols = slice(2 * hh * LANE, (2 * hh + 2) * LANE)
        return functools.reduce(jnp.add, [_dot(jnp.exp2(si - mx).astype(BF16), ve_ref[:, vcols])
                                          for si, ve_ref in zip(s, ve_refs)])

    for hh in range(heads):
        ocols = slice(hh * LANE, (hh + 1) * LANE)
        if kind == "mla":
            kcols = slice(hh * 2 * LANE, (hh + 1) * 2 * LANE)
            oe = attend(q_ref[:, kcols], kcols, hh)
            o = oe[:, :LANE] * (1.0 / oe[:, LANE:LANE + 1])
        else:
            qh = q_ref[:, ocols]
            lane = lax.broadcasted_iota(jnp.int32, (tq, LANE), 1)
            zero = jnp.zeros_like(qh)
            qs = jnp.concatenate([jnp.where(lane < DIFF_HEAD_DIM, qh, zero),
                                  jnp.where(lane >= DIFF_HEAD_DIM, qh, zero)], axis=0)
            oe = attend(qs, ocols, hh)
            o = (oe[:tq, :LANE] * (1.0 / oe[:tq, LANE:LANE + 1])
                 - oe[tq:, :LANE] * (lam / oe[tq:, LANE:LANE + 1]))
            o = _rms(o, gs_ref[...]) * (1.0 - lam_init)
        oh_ref[:, ocols] = o.astype(BF16)
    y = _dot(oh_ref[...], wo_ref[...])
    o_ref[...] = x_ref[...] + m[2:3] * _rms(y, g[1:2])


def _attn(x, mod, norm_g, q, pieces, wo, layer, seq, tq, kind, heads, extra=(), lam_init=0.0):
    n, d = x.shape
    tps = seq // tq
    qw = q.shape[-1]
    in_specs = [pl.BlockSpec((tq, d), lambda b, i: (b * tps + i, 0)),
                (pl.BlockSpec((1, 6, d), lambda b, i: (0, 0, 0)) if mod.shape[0] == 1
                 else pl.BlockSpec((1, 6, d), lambda b, i: (b, 0, 0))),
                pl.BlockSpec((None, 4, d), lambda b, i: (layer, 0, 0)),
                pl.BlockSpec((tq, qw), lambda b, i: (b * tps + i, 0))]
    args = [x, mod, norm_g, q]
    n_seq = n // seq
    scratch = [pltpu.VMEM((tq, d), BF16)]
    for kp, vp in pieces:
        lk = kp.shape[0] // n_seq
        in_specs += [pl.BlockSpec((lk, kp.shape[-1]), lambda b, i: (b, 0)),
                     pl.BlockSpec((lk, vp.shape[-1]), lambda b, i: (b, 0))]
        args += [kp, vp]
        scratch.append(pltpu.VMEM((lk, 2 * vp.shape[-1]), BF16))
    for e in extra:
        in_specs.append(pl.BlockSpec(e.shape, lambda b, i: (0, 0)))
        args.append(e)
    in_specs.append(_resident(wo.shape, lambda b, i: (0, 0)))
    args.append(wo)
    return pl.pallas_call(
        functools.partial(_attn_kernel, kind=kind, n_pieces=len(pieces), heads=heads, lam_init=lam_init),
        grid=(n_seq, tps),
        in_specs=in_specs,
        out_specs=pl.BlockSpec((tq, d), lambda b, i: (b * tps + i, 0)),
        out_shape=jax.ShapeDtypeStruct((n, d), F32),
        scratch_shapes=scratch,
        compiler_params=_params(2),
        name="attn_" + kind,
    )(*args)


def kernel(x, c, ctx, c_ctx, ada_w, ada_b, norm_g, ffn_w1, ffn_w2, pool_w, pool_scale, conv_in_w, conv_w,
           conv_out_w, mla_w_down, mla_g_q, mla_g_kv, mla_w_uq, mla_w_ukv, mla_w_o, diff_w_qkv,
           diff_lambda, diff_g_subln, diff_w_o):
    bsz, seq, d = x.shape
    lc = ctx.shape[1]
    depth = ada_w.shape[0]
    n_mixers = 4

    rows = bsz + 1
    rows_pad = -(-rows // 32) * 32
    s = jnp.concatenate([c, c_ctx[None, :], jnp.zeros((rows_pad - rows, d), F32)], axis=0)
    mod_all = _ada(s, ada_w, ada_b).reshape(depth, rows_pad, 6, d)

    w1 = ffn_w1.astype(BF16)
    w2 = ffn_w2.astype(BF16)
    xl = x.reshape(bsz * seq, d)
    xc = ctx.reshape(bsz * lc, d)
    t_lat = min(ROW_TILE_LATENT, seq)
    t_ctx = min(ROW_TILE_CONTEXT, lc)
    tq = min(QUERY_TILE, seq)

    for i in range(depth):
        kind = i % n_mixers
        j = i // n_mixers
        ctx_out = i < depth - 1
        mod_l = mod_all[i, :bsz]
        mod_c = mod_all[i, bsz:bsz + 1]
        if kind == 0:
            pw = pool_w[j].astype(BF16)
            ps = pool_scale[j][None, :]
            xl = _pool(xl, mod_l, norm_g, pw, ps, i, seq, t_lat)
            if ctx_out:
                xc = _pool(xc, mod_c, norm_g, pw, ps, i, lc, t_ctx)
        elif kind == 1:
            w_in = conv_in_w[j].astype(BF16)
            w_out = conv_out_w[j].astype(BF16)
            xl = _conv(xl, mod_l, norm_g, w_in, conv_w[j], w_out, i, seq, t_lat)
            if ctx_out:
                xc = _conv(xc, mod_c, norm_g, w_in, conv_w[j], w_out, i, lc, t_ctx)
        elif kind == 2:
            heads = mla_w_uq.shape[-1] // (MLA_NOPE + MLA_ROPE)
            q_lora = mla_g_q.shape[-1]
            kv_lora = mla_g_kv.shape[-1]
            scale = (MLA_NOPE + MLA_ROPE) ** -0.5 * LOG2E
            wdn = mla_w_down[j]
            wkr = wdn[:, q_lora + kv_lora:]
            zpad = jnp.zeros((d, LANE - MLA_ROPE), F32)
            wd = jnp.concatenate([wdn[:, :q_lora + kv_lora], wkr, zpad, _rot_cols(wkr, MLA_ROPE), zpad],
                                 axis=-1).astype(BF16)
            wq = mla_w_uq[j].reshape(q_lora, heads, MLA_NOPE + MLA_ROPE)
            wq_r = wq[:, :, MLA_NOPE:]
            zq = jnp.zeros((q_lora, heads, LANE - MLA_ROPE), F32)
            wuq = jnp.concatenate([
                wq[:, :, :MLA_NOPE].reshape(q_lora, heads * MLA_NOPE),
                jnp.concatenate([wq_r, zq], axis=-1).reshape(q_lora, heads * LANE),
                jnp.concatenate([_rot_cols(wq_r, MLA_ROPE), zq], axis=-1).reshape(q_lora, heads * LANE)],
                axis=-1).astype(BF16)
            wkv = mla_w_ukv[j].reshape(kv_lora, heads, MLA_NOPE + MLA_V)
            wukv = jnp.concatenate([wkv[:, :, :MLA_NOPE].reshape(kv_lora, heads * MLA_NOPE),
                                    wkv[:, :, MLA_NOPE:].reshape(kv_lora, heads * MLA_V)],
                                   axis=-1).astype(BF16)
            wo = mla_w_o[j].astype(BF16)
            gq = mla_g_q[j][None, :]
            gkv = mla_g_kv[j][None, :]
            cos_l, sin_l = _rope_table(seq, MLA_ROPE)
            cos_c, sin_c = _identity_rope(lc)
            ql, kl, vl = _mla_proj(xl, mod_l, norm_g, cos_l, sin_l, wd, gq, gkv, wuq, wukv, i, seq, t_lat,
                                   heads, scale)
            qc, kc, vc = _mla_proj(xc, mod_c, norm_g, cos_c, sin_c, wd, gq, gkv, wuq, wukv, i, lc, t_ctx,
                                   heads, scale)
            xl = _attn(xl, mod_l, norm_g, ql, [(kl, vl), (kc, vc)], wo, i, seq, tq, "mla", heads)
            if ctx_out:
                xc = _attn(xc, mod_c, norm_g, qc, [(kc, vc)], wo, i, lc, t_ctx, "mla", heads)
        else:
            heads = d // (2 * DIFF_HEAD_DIM)
            lam_init = 0.8 - 0.6 * math.exp(-0.3 * i)
            scale = DIFF_HEAD_DIM ** -0.5 * LOG2E
            wqkv = diff_w_qkv[j]
            wq, wk, wv = wqkv[:, :d], wqkv[:, d:2 * d], wqkv[:, 2 * d:]
            w_ext = jnp.concatenate([wq, _rot_cols(wq, DIFF_HEAD_DIM), wk, _rot_cols(wk, DIFF_HEAD_DIM), wv],
                                    axis=-1).astype(BF16)
            wo = diff_w_o[j].astype(BF16)
            cos_l, sin_l = _rope_table(seq, DIFF_HEAD_DIM)
            cos_c, sin_c = _identity_rope(lc)
            ql, kl, vl = _diff_proj(xl, mod_l, norm_g, cos_l, sin_l, w_ext, i, seq, t_lat, True, scale)
            outs_c = _diff_proj(xc, mod_c, norm_g, cos_c, sin_c, w_ext, i, lc, t_ctx, ctx_out, scale)
            kc, vc = outs_c[-2], outs_c[-1]
            extra = (diff_lambda[j], diff_g_subln[j][None, :])
            xl = _attn(xl, mod_l, norm_g, ql, [(kl, vl), (kc, vc)], wo, i, seq, tq, "diff", heads,
                       extra, lam_init)
            if ctx_out:
                xc = _attn(xc, mod_c, norm_g, outs_c[0], [(kc, vc)], wo, i, lc, t_ctx, "diff", heads,
                           extra, lam_init)
        xl = _ffn(xl, mod_l, norm_g, w1, w2, i)
        if ctx_out:
            xc = _ffn(xc, mod_c, norm_g, w1, w2, i)
    return xl.reshape(bsz, seq, d)
```

```python
import functools
import math

import jax
import jax.numpy as jnp
from jax import lax
from jax.experimental import pallas as pl
from jax.experimental.pallas import tpu as pltpu

F32 = jnp.float32
BF16 = jnp.bfloat16

GRID_W = 64
ROPE_BASE = 10000.0
NORM_EPS = 1e-6
LOG2E = math.log2(math.e)
POOL_WINDOWS = (2, 4, 8, 16)
CONV_WIDTH = 3
MLA_NOPE = 128
MLA_ROPE = 64
MLA_V = 128
DIFF_HEAD_DIM = 64
ROPE_AXIS_DIM = 32
LANE = 128
HALO_F32 = 8
HALO_BF16 = 16
VMEM_LIMIT = 56 * 1024 * 1024
ROW_TILE_LATENT = 512
ROW_TILE_CONTEXT = 256
QUERY_TILE = 256
FFN_ROW_TILE = 512
FFN_HIDDEN_CHUNK = 512


def _params(n_axes):
    return pltpu.CompilerParams(dimension_semantics=("arbitrary",) * n_axes,
                                vmem_limit_bytes=VMEM_LIMIT)


def _rms(x, g):
    return x * lax.rsqrt(jnp.mean(x * x, axis=-1, keepdims=True) + NORM_EPS) * g


def _dot(a, b):
    return jnp.dot(a, b, preferred_element_type=F32)


def _dot_nt(a, b):
    return lax.dot_general(a, b, (((1,), (1,)), ((), ())), preferred_element_type=F32)


def _ada_kernel(s_ref, w_ref, b_ref, o_ref):
    s = s_ref[...]
    a = (s / (1.0 + jnp.exp(-s))).astype(BF16)
    o_ref[...] = _dot(a, w_ref[...].astype(BF16)) + b_ref[...]


def _ada(s, ada_w, ada_b, tn=1536):
    depth, d, n = ada_w.shape
    rows = s.shape[0]
    return pl.pallas_call(
        _ada_kernel,
        grid=(depth, n // tn),
        in_specs=[pl.BlockSpec((rows, d), lambda l, j: (0, 0)),
                  pl.BlockSpec((None, d, tn), lambda l, j: (l, 0, j)),
                  pl.BlockSpec((None, 1, tn), lambda l, j: (l, 0, j))],
        out_specs=pl.BlockSpec((None, rows, tn), lambda l, j: (l, 0, j)),
        out_shape=jax.ShapeDtypeStruct((depth, rows, n), F32),
        compiler_params=_params(2),
        name="ada",
    )(s, ada_w, ada_b.reshape(depth, 1, n))


def _ffn_kernel(x_ref, mod_ref, g_ref, w1_ref, w2_ref, o_ref, h1_ref, *, fc):
    m = mod_ref[0]
    g = g_ref[...]
    x = x_ref[...]
    hf = (_rms(x, g[2:3]) * (1.0 + m[4:5]) + m[3:4]).astype(BF16)
    for c in range(h1_ref.shape[-1] // fc):
        cols = slice(c * fc, (c + 1) * fc)
        h1_ref[:, cols] = jnp.square(jnp.maximum(_dot(hf, w1_ref[:, cols]), 0.0)).astype(BF16)
    y = _dot(h1_ref[...], w2_ref[...])
    o_ref[...] = x + m[5:6] * _rms(y, g[3:4])


def _resident(block_shape, index_map):
    return pl.BlockSpec(block_shape, index_map, pipeline_mode=pl.Buffered(1))


def _ffn(x, mod, norm_g, w1, w2, layer, tm=FFN_ROW_TILE, fc=FFN_HIDDEN_CHUNK):
    n, d = x.shape
    dff = w1.shape[-1]
    tm = min(tm, n)
    n_tiles = n // tm
    tpm = n_tiles // mod.shape[0]
    return pl.pallas_call(
        functools.partial(_ffn_kernel, fc=fc),
        grid=(n_tiles,),
        in_specs=[pl.BlockSpec((tm, d), lambda i: (i, 0)),
                  pl.BlockSpec((1, 6, d), lambda i: (i // tpm, 0, 0)),
                  pl.BlockSpec((None, 4, d), lambda i: (layer, 0, 0)),
                  _resident((None, d, dff), lambda i: (layer, 0, 0)),
                  _resident((None, dff, d), lambda i: (layer, 0, 0))],
        out_specs=pl.BlockSpec((tm, d), lambda i: (i, 0)),
        out_shape=jax.ShapeDtypeStruct((n, d), F32),
        scratch_shapes=[pltpu.VMEM((tm, dff), BF16)],
        compiler_params=_params(1),
        name="ffn",
    )(x, mod, norm_g, w1, w2)


def _halo_specs(t, halo, n, d):
    r = t // halo
    last = n // halo - 1
    return [pl.BlockSpec((t, d), lambda i: (i, 0)),
            pl.BlockSpec((halo, d), lambda i: (jnp.maximum(i * r - 1, 0), 0)),
            pl.BlockSpec((halo, d), lambda i: (jnp.minimum((i + 1) * r, last), 0))]


def _mod_spec(mod, tps, d):
    if mod.shape[0] == 1:
        return pl.BlockSpec((1, 6, d), lambda i: (0, 0, 0))
    return pl.BlockSpec((1, 6, d), lambda i: (i // tps, 0, 0))


def _pool_kernel(x_ref, xp_ref, xn_ref, mod_ref, g_ref, pw_ref, ps_ref, o_ref, hs_ref, *, t, seq, tps):
    i = pl.program_id(0)
    tile = lax.rem(i, tps)
    first = tile == 0
    last = tile == tps - 1
    m = mod_ref[0]
    g = g_ref[...]

    def hmod(xx):
        return _rms(xx, g[0:1]) * (1.0 + m[1:2]) + m[0:1]

    x = x_ref[...]
    h = hmod(x)
    hb = HALO_F32
    hs_ref[0:hb, :] = jnp.where(first, 0.0, hmod(xp_ref[...]))
    hs_ref[hb:hb + t, :] = h
    hs_ref[hb + t:2 * hb + t, :] = jnp.where(last, 0.0, hmod(xn_ref[...]))
    pos = tile * t + lax.broadcasted_iota(jnp.int32, (t, 1), 0)
    grp = pw_ref.shape[-1]
    ys = []
    for gi, w in enumerate(POOL_WINDOWS):
        lo = w // 2
        hi = w - 1 - lo
        cols = slice(gi * grp, (gi + 1) * grp)
        acc = hs_ref[hb - lo:hb - lo + t, cols]
        for o in range(-lo + 1, hi + 1):
            acc = acc + hs_ref[hb + o:hb + o + t, cols]
        cnt = (jnp.minimum(pos + hi + 1, seq) - jnp.maximum(pos - lo, 0)).astype(F32)
        dlt = acc * (1.0 / cnt) - h[:, cols]
        ys.append(_dot(dlt.astype(BF16), pw_ref[gi]))
    y = jnp.concatenate(ys, axis=-1) * ps_ref[...]
    o_ref[...] = x + m[2:3] * _rms(y, g[1:2])


def _pool(x, mod, norm_g, pool_w, pool_scale, layer, seq, t):
    n, d = x.shape
    tps = seq // t
    ng, grp, _ = pool_w.shape
    return pl.pallas_call(
        functools.partial(_pool_kernel, t=t, seq=seq, tps=tps),
        grid=(n // t,),
        in_specs=_halo_specs(t, HALO_F32, n, d) + [
            _mod_spec(mod, tps, d),
            pl.BlockSpec((None, 4, d), lambda i: (layer, 0, 0)),
            pl.BlockSpec((ng, grp, grp), lambda i: (0, 0, 0)),
            pl.BlockSpec((1, d), lambda i: (0, 0))],
        out_specs=pl.BlockSpec((t, d), lambda i: (i, 0)),
        out_shape=jax.ShapeDtypeStruct((n, d), F32),
        scratch_shapes=[pltpu.VMEM((t + 2 * HALO_F32, d), F32)],
        compiler_params=_params(1),
        name="pool",
    )(x, x, x, mod, norm_g, pool_w, pool_scale)


def _conv_kernel(x_ref, xp_ref, xn_ref, mod_ref, g_ref, win_ref, cw_ref, wout_ref, o_ref,
                 hs_ref, u_ref, *, t, tps):
    i = pl.program_id(0)
    tile = lax.rem(i, tps)
    first = tile == 0
    last = tile == tps - 1
    m = mod_ref[0]
    g = g_ref[...]
    d = x_ref.shape[-1]

    def hmod(xx):
        return _rms(xx, g[0:1]) * (1.0 + m[1:2]) + m[0:1]

    x = x_ref[...]
    hb = HALO_BF16
    hs_ref[0:hb, :] = jnp.where(first, 0.0, hmod(xp_ref[...])).astype(BF16)
    hs_ref[hb:hb + t, :] = hmod(x).astype(BF16)
    hs_ref[hb + t:2 * hb + t, :] = jnp.where(last, 0.0, hmod(xn_ref[...])).astype(BF16)
    hs = hs_ref[...]
    u_ref[...] = _dot(hs, win_ref[:, d:2 * d]) * _dot(hs, win_ref[:, 2 * d:3 * d])
    cw = cw_ref[...]
    z = (u_ref[hb - 1:hb - 1 + t, :] * cw[0:1] + u_ref[hb:hb + t, :] * cw[1:2]
         + u_ref[hb + 1:hb + 1 + t, :] * cw[2:3])
    b_gate = _dot(hs_ref[hb:hb + t, :], win_ref[:, 0:d])
    y = _dot((b_gate * z).astype(BF16), wout_ref[...])
    o_ref[...] = x + m[2:3] * _rms(y, g[1:2])


def _conv(x, mod, norm_g, w_in, conv_w, w_out, layer, seq, t):
    n, d = x.shape
    tps = seq // t
    return pl.pallas_call(
        functools.partial(_conv_kernel, t=t, tps=tps),
        grid=(n // t,),
        in_specs=_halo_specs(t, HALO_BF16, n, d) + [
            _mod_spec(mod, tps, d),
            pl.BlockSpec((None, 4, d), lambda i: (layer, 0, 0)),
            pl.BlockSpec((d, 3 * d), lambda i: (0, 0)),
            pl.BlockSpec((CONV_WIDTH, d), lambda i: (0, 0)),
            pl.BlockSpec((d, d), lambda i: (0, 0))],
        out_specs=pl.BlockSpec((t, d), lambda i: (i, 0)),
        out_shape=jax.ShapeDtypeStruct((n, d), F32),
        scratch_shapes=[pltpu.VMEM((t + 2 * HALO_BF16, d), BF16),
                        pltpu.VMEM((t + 2 * HALO_BF16, d), F32)],
        compiler_params=_params(1),
        name="conv",
    )(x, x, x, mod, norm_g, w_in, conv_w, w_out)


def _rope_table(seq):
    tpos = jnp.arange(seq)
    row = (tpos // GRID_W).astype(F32)
    col = (tpos % GRID_W).astype(F32)
    inv_freq = ROPE_BASE ** (-jnp.arange(0, ROPE_AXIS_DIM, 2, dtype=F32) / ROPE_AXIS_DIM)
    a = jnp.concatenate([row[:, None] * inv_freq[None, :], col[:, None] * inv_freq[None, :]], axis=-1)
    a = jnp.tile(a, (1, LANE // ROPE_AXIS_DIM))
    sign = jnp.where(jnp.arange(LANE) < LANE // 2, -1.0, 1.0).astype(F32)
    return jnp.cos(a), jnp.sin(a) * sign


def _identity_rope(seq):
    return jnp.ones((seq, LANE), F32), jnp.zeros((seq, LANE), F32)


def _pair_halves(w):
    half = ROPE_AXIS_DIM // 2
    shp = w.shape
    groups = shp[-1] // (4 * half)
    wg = w.reshape(shp[:-1] + (groups, 2, 2, half))
    first = wg[..., 0, :].reshape(shp[:-1] + (groups, 2 * half))
    second = wg[..., 1, :].reshape(shp[:-1] + (groups, 2 * half))
    return first, second


def _rope(y, cos, sin):
    return y * cos + pltpu.roll(y, LANE // 2, 1) * sin


def _mla_proj_kernel(x_ref, mod_ref, g_ref, cos_ref, sin_ref, wd_ref, gq_ref, gkv_ref, wuq_ref,
                     wukv_ref, q_ref, k_ref, vt_ref, *, heads, q_lora, kv_lora, scale):
    m = mod_ref[0]
    g = g_ref[...]
    h = (_rms(x_ref[...], g[0:1]) * (1.0 + m[1:2]) + m[0:1]).astype(BF16)
    a = _dot(h, wd_ref[...])
    cos = cos_ref[...]
    sin = sin_ref[...]
    c0 = q_lora + kv_lora
    kr = _rope(a[:, c0:c0 + LANE], cos, sin).astype(BF16)
    cq = _rms(a[:, :q_lora], gq_ref[...]).astype(BF16)
    ckv = _rms(a[:, q_lora:c0], gkv_ref[...]).astype(BF16)
    qa = _dot(cq, wuq_ref[...])
    kv = _dot(ckv, wukv_ref[...])
    hw = heads * LANE
    for hh in range(heads):
        lo = hh * LANE
        qn = qa[:, lo:lo + LANE]
        qr = _rope(qa[:, hw + lo:hw + lo + LANE], cos, sin)
        q_ref[:, 2 * lo:2 * lo + LANE] = (qn * scale).astype(BF16)
        q_ref[:, 2 * lo + LANE:2 * lo + 2 * LANE] = (qr * scale).astype(BF16)
        k_ref[:, 2 * lo:2 * lo + LANE] = kv[:, lo:lo + LANE].astype(BF16)
        k_ref[:, 2 * lo + LANE:2 * lo + 2 * LANE] = kr
    vt_ref[...] = kv[:, hw:].T.astype(BF16)


def _mla_proj(x, mod, norm_g, cos, sin, wd, gq, gkv, wuq, wukv, layer, seq, t, heads, scale):
    n, d = x.shape
    tps = seq // t
    q_lora = gq.shape[-1]
    kv_lora = gkv.shape[-1]
    qw = heads * 2 * LANE
    return pl.pallas_call(
        functools.partial(_mla_proj_kernel, heads=heads, q_lora=q_lora, kv_lora=kv_lora, scale=scale),
        grid=(n // t,),
        in_specs=[pl.BlockSpec((t, d), lambda i: (i, 0)),
                  _mod_spec(mod, tps, d),
                  pl.BlockSpec((None, 4, d), lambda i: (layer, 0, 0)),
                  pl.BlockSpec((t, LANE), lambda i: (i % tps, 0)),
                  pl.BlockSpec((t, LANE), lambda i: (i % tps, 0)),
                  pl.BlockSpec(wd.shape, lambda i: (0, 0)),
                  pl.BlockSpec((1, q_lora), lambda i: (0, 0)),
                  pl.BlockSpec((1, kv_lora), lambda i: (0, 0)),
                  pl.BlockSpec(wuq.shape, lambda i: (0, 0)),
                  pl.BlockSpec(wukv.shape, lambda i: (0, 0))],
        out_specs=[pl.BlockSpec((t, qw), lambda i: (i, 0)),
                   pl.BlockSpec((t, qw), lambda i: (i, 0)),
                   pl.BlockSpec((None, heads * LANE, t), lambda i: (i // tps, 0, i % tps))],
        out_shape=[jax.ShapeDtypeStruct((n, qw), BF16),
                   jax.ShapeDtypeStruct((n, qw), BF16),
                   jax.ShapeDtypeStruct((n // seq, heads * LANE, seq), BF16)],
        compiler_params=_params(1),
        name="mla_proj",
    )(x, mod, norm_g, cos, sin, wd, gq, gkv, wuq, wukv)


def _diff_proj_kernel(x_ref, mod_ref, g_ref, cos_ref, sin_ref, w_ref, *out_refs, need_q, scale):
    m = mod_ref[0]
    g = g_ref[...]
    d = x_ref.shape[-1]
    h = (_rms(x_ref[...], g[0:1]) * (1.0 + m[1:2]) + m[0:1]).astype(BF16)
    cos = cos_ref[...]
    sin = sin_ref[...]

    def roped(col0, ref, mul):
        y = _dot(h, w_ref[:, col0:col0 + d])
        for j in range(d // LANE):
            lo = j * LANE
            r = _rope(y[:, lo:lo + LANE], cos, sin)
            ref[:, lo:lo + LANE] = (r * mul).astype(BF16) if mul != 1.0 else r.astype(BF16)

    if need_q:
        q_ref, k_ref, vt_ref = out_refs
        roped(0, q_ref, scale)
    else:
        k_ref, vt_ref = out_refs
    roped(d, k_ref, 1.0)
    vt_ref[...] = _dot(h, w_ref[:, 2 * d:3 * d]).T.astype(BF16)


def _diff_proj(x, mod, norm_g, cos, sin, w_ext, layer, seq, t, need_q, scale):
    n, d = x.shape
    tps = seq // t
    n_out = 3 if need_q else 2
    return pl.pallas_call(
        functools.partial(_diff_proj_kernel, need_q=need_q, scale=scale),
        grid=(n // t,),
        in_specs=[pl.BlockSpec((t, d), lambda i: (i, 0)),
                  _mod_spec(mod, tps, d),
                  pl.BlockSpec((None, 4, d), lambda i: (layer, 0, 0)),
                  pl.BlockSpec((t, LANE), lambda i: (i % tps, 0)),
                  pl.BlockSpec((t, LANE), lambda i: (i % tps, 0)),
                  pl.BlockSpec(w_ext.shape, lambda i: (0, 0))],
        out_specs=[pl.BlockSpec((t, d), lambda i: (i, 0))] * (n_out - 1)
        + [pl.BlockSpec((None, d, t), lambda i: (i // tps, 0, i % tps))],
        out_shape=[jax.ShapeDtypeStruct((n, d), BF16)] * (n_out - 1)
        + [jax.ShapeDtypeStruct((n // seq, d, seq), BF16)],
        compiler_params=_params(1),
        name="diff_proj",
    )(x, mod, norm_g, cos, sin, w_ext)


def _attn_kernel(*refs, kind, n_pieces, heads, lam_init):
    x_ref, mod_ref, g_ref, q_ref = refs[:4]
    kv_refs = refs[4:4 + 2 * n_pieces]
    k_refs = kv_refs[0::2]
    vt_refs = kv_refs[1::2]
    rest = refs[4 + 2 * n_pieces:]
    if kind == "diff":
        lam_ref, gs_ref, wo_ref, o_ref, oh_ref = rest[:5]
        lv = lam_ref[...]
        lam = (jnp.exp(jnp.sum(lv[0:1] * lv[1:2], axis=-1, keepdims=True))
               - jnp.exp(jnp.sum(lv[2:3] * lv[3:4], axis=-1, keepdims=True)) + lam_init)
    else:
        wo_ref, o_ref, oh_ref = rest[:3]
    ve_refs = rest[-n_pieces:]
    m = mod_ref[0]
    g = g_ref[...]
    tq = q_ref.shape[0]
    ver = LANE + HALO_BF16

    @pl.when(pl.program_id(1) == 0)
    def _():
        for vt_ref, ve_ref in zip(vt_refs, ve_refs):
            ones = jnp.ones((HALO_BF16, vt_ref.shape[-1]), BF16)
            for hh in range(heads):
                ve_ref[hh * ver:hh * ver + LANE, :] = vt_ref[hh * LANE:(hh + 1) * LANE, :]
                ve_ref[hh * ver + LANE:(hh + 1) * ver, :] = ones

    def scores(hh):
        if kind == "mla":
            kcols = slice(hh * 2 * LANE, (hh + 1) * 2 * LANE)
            qs = q_ref[:, kcols]
        else:
            kcols = slice(hh * LANE, (hh + 1) * LANE)
            qh = q_ref[:, kcols]
            lane = lax.broadcasted_iota(jnp.int32, (tq, LANE), 1)
            in_map0 = lax.rem(lane, 2 * ROPE_AXIS_DIM) < ROPE_AXIS_DIM
            zero = jnp.zeros_like(qh)
            qs = jnp.concatenate([jnp.where(in_map0, qh, zero), jnp.where(in_map0, zero, qh)], axis=0)
        return [_dot_nt(k_ref[:, kcols], qs) for k_ref in k_refs]

    def weighted_values(st, hh):
        mx = functools.reduce(jnp.maximum, [jnp.max(si, axis=0, keepdims=True) for si in st])
        return functools.reduce(jnp.add, [_dot(ve_ref[hh * ver:(hh + 1) * ver, :],
                                               jnp.exp2(si - mx).astype(BF16))
                                          for si, ve_ref in zip(st, ve_refs)])

    st_next = scores(0)
    for hh in range(heads):
        ocols = slice(hh * LANE, (hh + 1) * LANE)
        st = st_next
        if hh + 1 < heads:
            st_next = scores(hh + 1)
        oe = weighted_values(st, hh)
        if kind == "mla":
            ot = oe[:LANE] * (1.0 / oe[LANE:LANE + 1])
        else:
            ot = (oe[:LANE, :tq] * (1.0 / oe[LANE:LANE + 1, :tq])
                  - oe[:LANE, tq:] * (lam / oe[LANE:LANE + 1, tq:]))
            ot = (ot * lax.rsqrt(jnp.mean(ot * ot, axis=0, keepdims=True) + NORM_EPS) * gs_ref[...]
                  * (1.0 - lam_init))
        oh_ref[:, ocols] = ot.T.astype(BF16)
    y = _dot(oh_ref[...], wo_ref[...])
    o_ref[...] = x_ref[...] + m[2:3] * _rms(y, g[1:2])


def _attn(x, mod, norm_g, q, pieces, wo, layer, seq, tq, kind, heads, extra=(), lam_init=0.0):
    n, d = x.shape
    tps = seq // tq
    qw = q.shape[-1]
    in_specs = [pl.BlockSpec((tq, d), lambda b, i: (b * tps + i, 0)),
                (pl.BlockSpec((1, 6, d), lambda b, i: (0, 0, 0)) if mod.shape[0] == 1
                 else pl.BlockSpec((1, 6, d), lambda b, i: (b, 0, 0))),
                pl.BlockSpec((None, 4, d), lambda b, i: (layer, 0, 0)),
                pl.BlockSpec((tq, qw), lambda b, i: (b * tps + i, 0))]
    args = [x, mod, norm_g, q]
    n_seq = n // seq
    scratch = [pltpu.VMEM((tq, d), BF16)]
    for kp, vtp in pieces:
        lk = vtp.shape[-1]
        in_specs += [pl.BlockSpec((lk, kp.shape[-1]), lambda b, i: (b, 0)),
                     pl.BlockSpec((None, vtp.shape[1], lk), lambda b, i: (b, 0, 0))]
        args += [kp, vtp]
        scratch.append(pltpu.VMEM((heads * (LANE + HALO_BF16), lk), BF16))
    for e in extra:
        in_specs.append(pl.BlockSpec(e.shape, lambda b, i: (0, 0)))
        args.append(e)
    in_specs.append(_resident(wo.shape, lambda b, i: (0, 0)))
    args.append(wo)
    return pl.pallas_call(
        functools.partial(_attn_kernel, kind=kind, n_pieces=len(pieces), heads=heads, lam_init=lam_init),
        grid=(n_seq, tps),
        in_specs=in_specs,
        out_specs=pl.BlockSpec((tq, d), lambda b, i: (b * tps + i, 0)),
        out_shape=jax.ShapeDtypeStruct((n, d), F32),
        scratch_shapes=scratch,
        compiler_params=_params(2),
        name="attn_" + kind,
    )(*args)


def kernel(x, c, ctx, c_ctx, ada_w, ada_b, norm_g, ffn_w1, ffn_w2, pool_w, pool_scale, conv_in_w, conv_w,
           conv_out_w, mla_w_down, mla_g_q, mla_g_kv, mla_w_uq, mla_w_ukv, mla_w_o, diff_w_qkv,
           diff_lambda, diff_g_subln, diff_w_o):
    bsz, seq, d = x.shape
    lc = ctx.shape[1]
    depth = ada_w.shape[0]
    n_mixers = 4

    rows = bsz + 1
    rows_pad = -(-rows // 32) * 32
    s = jnp.concatenate([c, c_ctx[None, :], jnp.zeros((rows_pad - rows, d), F32)], axis=0)
    mod_all = _ada(s, ada_w, ada_b).reshape(depth, rows_pad, 6, d)

    w1 = ffn_w1.astype(BF16)
    w2 = ffn_w2.astype(BF16)
    xl = x.reshape(bsz * seq, d)
    xc = ctx.reshape(bsz * lc, d)
    t_lat = min(ROW_TILE_LATENT, seq)
    t_ctx = min(ROW_TILE_CONTEXT, lc)
    tq = min(QUERY_TILE, seq)

    for i in range(depth):
        kind = i % n_mixers
        j = i // n_mixers
        ctx_out = i < depth - 1
        mod_l = mod_all[i, :bsz]
        mod_c = mod_all[i, bsz:bsz + 1]
        if kind == 0:
            pw = pool_w[j].astype(BF16)
            ps = pool_scale[j][None, :]
            xl = _pool(xl, mod_l, norm_g, pw, ps, i, seq, t_lat)
            if ctx_out:
                xc = _pool(xc, mod_c, norm_g, pw, ps, i, lc, t_ctx)
        elif kind == 1:
            w_in = conv_in_w[j].astype(BF16)
            w_out = conv_out_w[j].astype(BF16)
            xl = _conv(xl, mod_l, norm_g, w_in, conv_w[j], w_out, i, seq, t_lat)
            if ctx_out:
                xc = _conv(xc, mod_c, norm_g, w_in, conv_w[j], w_out, i, lc, t_ctx)
        elif kind == 2:
            heads = mla_w_uq.shape[-1] // (MLA_NOPE + MLA_ROPE)
            q_lora = mla_g_q.shape[-1]
            kv_lora = mla_g_kv.shape[-1]
            scale = (MLA_NOPE + MLA_ROPE) ** -0.5 * LOG2E
            wdn = mla_w_down[j]
            kr1, kr2 = _pair_halves(wdn[:, q_lora + kv_lora:])
            zk = jnp.zeros_like(kr1)
            wd = jnp.concatenate([wdn[:, :q_lora + kv_lora],
                                  jnp.concatenate([kr1, zk, kr2, zk], axis=-1).reshape(d, LANE)],
                                 axis=-1).astype(BF16)
            wq = mla_w_uq[j].reshape(q_lora, heads, MLA_NOPE + MLA_ROPE)
            qr1, qr2 = _pair_halves(wq[:, :, MLA_NOPE:])
            zq = jnp.zeros_like(qr1)
            wuq = jnp.concatenate([
                wq[:, :, :MLA_NOPE].reshape(q_lora, heads * MLA_NOPE),
                jnp.concatenate([qr1, zq, qr2, zq], axis=-1).reshape(q_lora, heads * LANE)],
                axis=-1).astype(BF16)
            wkv = mla_w_ukv[j].reshape(kv_lora, heads, MLA_NOPE + MLA_V)
            wukv = jnp.concatenate([wkv[:, :, :MLA_NOPE].reshape(kv_lora, heads * MLA_NOPE),
                                    wkv[:, :, MLA_NOPE:].reshape(kv_lora, heads * MLA_V)],
                                   axis=-1).astype(BF16)
            wo = mla_w_o[j].astype(BF16)
            gq = mla_g_q[j][None, :]
            gkv = mla_g_kv[j][None, :]
            cos_l, sin_l = _rope_table(seq)
            cos_c, sin_c = _identity_rope(lc)
            ql, kl, vl = _mla_proj(xl, mod_l, norm_g, cos_l, sin_l, wd, gq, gkv, wuq, wukv, i, seq, t_lat,
                                   heads, scale)
            qc, kc, vc = _mla_proj(xc, mod_c, norm_g, cos_c, sin_c, wd, gq, gkv, wuq, wukv, i, lc, t_ctx,
                                   heads, scale)
            xl = _attn(xl, mod_l, norm_g, ql, [(kl, vl), (kc, vc)], wo, i, seq, tq, "mla", heads)
            if ctx_out:
                xc = _attn(xc, mod_c, norm_g, qc, [(kc, vc)], wo, i, lc, t_ctx, "mla", heads)
        else:
            heads = d // (2 * DIFF_HEAD_DIM)
            lam_init = 0.8 - 0.6 * math.exp(-0.3 * i)
            scale = DIFF_HEAD_DIM ** -0.5 * LOG2E
            wqkv = diff_w_qkv[j]
            wq, wk, wv = wqkv[:, :d], wqkv[:, d:2 * d], wqkv[:, 2 * d:]

            def paired(w):
                first, second = _pair_halves(w.reshape(d, heads, 2 * DIFF_HEAD_DIM))
                return jnp.concatenate([first.reshape(d, heads, DIFF_HEAD_DIM),
                                        second.reshape(d, heads, DIFF_HEAD_DIM)], axis=-1).reshape(d, d)

            w_ext = jnp.concatenate([paired(wq), paired(wk), wv], axis=-1).astype(BF16)
            wo = diff_w_o[j].astype(BF16)
            cos_l, sin_l = _rope_table(seq)
            cos_c, sin_c = _identity_rope(lc)
            ql, kl, vl = _diff_proj(xl, mod_l, norm_g, cos_l, sin_l, w_ext, i, seq, t_lat, True, scale)
            outs_c = _diff_proj(xc, mod_c, norm_g, cos_c, sin_c, w_ext, i, lc, t_ctx, ctx_out, scale)
            kc, vc = outs_c[-2], outs_c[-1]
            extra = (diff_lambda[j], diff_g_subln[j][:, None])
            xl = _attn(xl, mod_l, norm_g, ql, [(kl, vl), (kc, vc)], wo, i, seq, tq, "diff", heads,
                       extra, lam_init)
            if ctx_out:
                xc = _attn(xc, mod_c, norm_g, outs_c[0], [(kc, vc)], wo, i, lc, t_ctx, "diff", heads,
                           extra, lam_init)
        xl = _ffn(xl, mod_l, norm_g, w1, w2, i)
        if ctx_out:
            xc = _ffn(xc, mod_c, norm_g, w1, w2, i)
    return xl.reshape(bsz, seq, d)
```

```python
import functools
import math

import jax
import jax.numpy as jnp
from jax import lax
from jax.experimental import pallas as pl
from jax.experimental.pallas import tpu as pltpu

F32 = jnp.float32
BF16 = jnp.bfloat16

GRID_W = 64
ROPE_BASE = 10000.0
NORM_EPS = 1e-6
LOG2E = math.log2(math.e)
POOL_WINDOWS = (2, 4, 8, 16)
CONV_WIDTH = 3
MLA_NOPE = 128
MLA_ROPE = 64
MLA_V = 128
DIFF_HEAD_DIM = 64
ROPE_AXIS_DIM = 32
LANE = 128
HALO_F32 = 8
HALO_BF16 = 16
VMEM_LIMIT = 60 * 1024 * 1024
ROW_TILE_LATENT = 512
ROW_TILE_CONTEXT = 256
SOFTMAX_ROWS = 512
FFN_ROW_TILE = 512
FFN_HIDDEN_CHUNK = 512


def _params(n_axes):
    return pltpu.CompilerParams(dimension_semantics=("arbitrary",) * n_axes,
                                vmem_limit_bytes=VMEM_LIMIT)


def _rms(x, g):
    return x * lax.rsqrt(jnp.mean(x * x, axis=-1, keepdims=True) + NORM_EPS) * g


def _dot(a, b):
    return jnp.dot(a, b, preferred_element_type=F32)


def _dot_nt(a, b):
    return lax.dot_general(a, b, (((1,), (1,)), ((), ())), preferred_element_type=F32)


def _ada_kernel(s_ref, w_ref, b_ref, o_ref):
    s = s_ref[...]
    a = (s / (1.0 + jnp.exp(-s))).astype(BF16)
    o_ref[...] = _dot(a, w_ref[...].astype(BF16)) + b_ref[...]


def _ada(s, ada_w, ada_b, tn=1536):
    depth, d, n = ada_w.shape
    rows = s.shape[0]
    return pl.pallas_call(
        _ada_kernel,
        grid=(depth, n // tn),
        in_specs=[pl.BlockSpec((rows, d), lambda l, j: (0, 0)),
                  pl.BlockSpec((None, d, tn), lambda l, j: (l, 0, j)),
                  pl.BlockSpec((None, 1, tn), lambda l, j: (l, 0, j))],
        out_specs=pl.BlockSpec((None, rows, tn), lambda l, j: (l, 0, j)),
        out_shape=jax.ShapeDtypeStruct((depth, rows, n), F32),
        compiler_params=_params(2),
        name="ada",
    )(s, ada_w, ada_b.reshape(depth, 1, n))


def _ffn_kernel(x_ref, mod_ref, g_ref, w1_ref, w2_ref, o_ref, h1_ref, *, fc):
    m = mod_ref[0]
    g = g_ref[...]
    x = x_ref[...]
    hf = (_rms(x, g[2:3]) * (1.0 + m[4:5]) + m[3:4]).astype(BF16)
    for c in range(h1_ref.shape[-1] // fc):
        cols = slice(c * fc, (c + 1) * fc)
        h1_ref[:, cols] = jnp.square(jnp.maximum(_dot(hf, w1_ref[:, cols]), 0.0)).astype(BF16)
    y = _dot(h1_ref[...], w2_ref[...])
    o_ref[...] = x + m[5:6] * _rms(y, g[3:4])


def _resident(block_shape, index_map):
    return pl.BlockSpec(block_shape, index_map, pipeline_mode=pl.Buffered(1))


def _ffn(x, mod, norm_g, w1, w2, layer, tm=FFN_ROW_TILE, fc=FFN_HIDDEN_CHUNK):
    n, d = x.shape
    dff = w1.shape[-1]
    tm = min(tm, n)
    n_tiles = n // tm
    tpm = n_tiles // mod.shape[0]
    return pl.pallas_call(
        functools.partial(_ffn_kernel, fc=fc),
        grid=(n_tiles,),
        in_specs=[pl.BlockSpec((tm, d), lambda i: (i, 0)),
                  pl.BlockSpec((1, 6, d), lambda i: (i // tpm, 0, 0)),
                  pl.BlockSpec((None, 4, d), lambda i: (layer, 0, 0)),
                  _resident((None, d, dff), lambda i: (layer, 0, 0)),
                  _resident((None, dff, d), lambda i: (layer, 0, 0))],
        out_specs=pl.BlockSpec((tm, d), lambda i: (i, 0)),
        out_shape=jax.ShapeDtypeStruct((n, d), F32),
        scratch_shapes=[pltpu.VMEM((tm, dff), BF16)],
        compiler_params=_params(1),
        name="ffn",
    )(x, mod, norm_g, w1, w2)


def _halo_specs(t, halo, n, d):
    r = t // halo
    last = n // halo - 1
    return [pl.BlockSpec((t, d), lambda i: (i, 0)),
            pl.BlockSpec((halo, d), lambda i: (jnp.maximum(i * r - 1, 0), 0)),
            pl.BlockSpec((halo, d), lambda i: (jnp.minimum((i + 1) * r, last), 0))]


def _mod_spec(mod, tps, d):
    if mod.shape[0] == 1:
        return pl.BlockSpec((1, 6, d), lambda i: (0, 0, 0))
    return pl.BlockSpec((1, 6, d), lambda i: (i // tps, 0, 0))


def _pool_kernel(x_ref, xp_ref, xn_ref, mod_ref, g_ref, pw_ref, ps_ref, o_ref, hs_ref, *, t, seq, tps):
    i = pl.program_id(0)
    tile = lax.rem(i, tps)
    first = tile == 0
    last = tile == tps - 1
    m = mod_ref[0]
    g = g_ref[...]

    def hmod(xx):
        return _rms(xx, g[0:1]) * (1.0 + m[1:2]) + m[0:1]

    x = x_ref[...]
    h = hmod(x)
    hb = HALO_F32
    hs_ref[0:hb, :] = jnp.where(first, 0.0, hmod(xp_ref[...]))
    hs_ref[hb:hb + t, :] = h
    hs_ref[hb + t:2 * hb + t, :] = jnp.where(last, 0.0, hmod(xn_ref[...]))
    pos = tile * t + lax.broadcasted_iota(jnp.int32, (t, 1), 0)
    grp = pw_ref.shape[-1]
    ys = []
    for gi, w in enumerate(POOL_WINDOWS):
        lo = w // 2
        hi = w - 1 - lo
        cols = slice(gi * grp, (gi + 1) * grp)
        rows = t + 2 * hb
        run = hs_ref[:, cols]
        span = 1
        while span < w:
            run = run + pltpu.roll(run, rows - span, 0)
            span *= 2
        start = hb - lo
        acc = (pltpu.roll(run, rows - start, 0) if start else run)[:t]
        cnt = (jnp.minimum(pos + hi + 1, seq) - jnp.maximum(pos - lo, 0)).astype(F32)
        dlt = acc * (1.0 / cnt) - h[:, cols]
        ys.append(_dot(dlt.astype(BF16), pw_ref[gi]))
    y = jnp.concatenate(ys, axis=-1) * ps_ref[...]
    o_ref[...] = x + m[2:3] * _rms(y, g[1:2])


def _pool(x, mod, norm_g, pool_w, pool_scale, layer, seq, t):
    n, d = x.shape
    tps = seq // t
    ng, grp, _ = pool_w.shape
    return pl.pallas_call(
        functools.partial(_pool_kernel, t=t, seq=seq, tps=tps),
        grid=(n // t,),
        in_specs=_halo_specs(t, HALO_F32, n, d) + [
            _mod_spec(mod, tps, d),
            pl.BlockSpec((None, 4, d), lambda i: (layer, 0, 0)),
            pl.BlockSpec((ng, grp, grp), lambda i: (0, 0, 0)),
            pl.BlockSpec((1, d), lambda i: (0, 0))],
        out_specs=pl.BlockSpec((t, d), lambda i: (i, 0)),
        out_shape=jax.ShapeDtypeStruct((n, d), F32),
        scratch_shapes=[pltpu.VMEM((t + 2 * HALO_F32, d), F32)],
        compiler_params=_params(1),
        name="pool",
    )(x, x, x, mod, norm_g, pool_w, pool_scale)


def _conv_kernel(x_ref, xp_ref, xn_ref, mod_ref, g_ref, win_ref, cw_ref, wout_ref, o_ref,
                 hs_ref, u_ref, *, t, tps):
    i = pl.program_id(0)
    tile = lax.rem(i, tps)
    first = tile == 0
    last = tile == tps - 1
    m = mod_ref[0]
    g = g_ref[...]
    d = x_ref.shape[-1]

    def hmod(xx):
        return _rms(xx, g[0:1]) * (1.0 + m[1:2]) + m[0:1]

    x = x_ref[...]
    hb = HALO_BF16
    hs_ref[0:hb, :] = jnp.where(first, 0.0, hmod(xp_ref[...])).astype(BF16)
    hs_ref[hb:hb + t, :] = hmod(x).astype(BF16)
    hs_ref[hb + t:2 * hb + t, :] = jnp.where(last, 0.0, hmod(xn_ref[...])).astype(BF16)
    hs = hs_ref[...]
    u_ref[...] = _dot(hs, win_ref[:, d:2 * d]) * _dot(hs, win_ref[:, 2 * d:3 * d])
    cw = cw_ref[...]
    z = (u_ref[hb - 1:hb - 1 + t, :] * cw[0:1] + u_ref[hb:hb + t, :] * cw[1:2]
         + u_ref[hb + 1:hb + 1 + t, :] * cw[2:3])
    b_gate = _dot(hs_ref[hb:hb + t, :], win_ref[:, 0:d])
    y = _dot((b_gate * z).astype(BF16), wout_ref[...])
    o_ref[...] = x + m[2:3] * _rms(y, g[1:2])


def _conv(x, mod, norm_g, w_in, conv_w, w_out, layer, seq, t):
    n, d = x.shape
    tps = seq // t
    return pl.pallas_call(
        functools.partial(_conv_kernel, t=t, tps=tps),
        grid=(n // t,),
        in_specs=_halo_specs(t, HALO_BF16, n, d) + [
            _mod_spec(mod, tps, d),
            pl.BlockSpec((None, 4, d), lambda i: (layer, 0, 0)),
            pl.BlockSpec((d, 3 * d), lambda i: (0, 0)),
            pl.BlockSpec((CONV_WIDTH, d), lambda i: (0, 0)),
            pl.BlockSpec((d, d), lambda i: (0, 0))],
        out_specs=pl.BlockSpec((t, d), lambda i: (i, 0)),
        out_shape=jax.ShapeDtypeStruct((n, d), F32),
        scratch_shapes=[pltpu.VMEM((t + 2 * HALO_BF16, d), BF16),
                        pltpu.VMEM((t + 2 * HALO_BF16, d), F32)],
        compiler_params=_params(1),
        name="conv",
    )(x, x, x, mod, norm_g, w_in, conv_w, w_out)


def _rope_table(seq):
    tpos = jnp.arange(seq)
    row = (tpos // GRID_W).astype(F32)
    col = (tpos % GRID_W).astype(F32)
    inv_freq = ROPE_BASE ** (-jnp.arange(0, ROPE_AXIS_DIM, 2, dtype=F32) / ROPE_AXIS_DIM)
    a = jnp.concatenate([row[:, None] * inv_freq[None, :], col[:, None] * inv_freq[None, :]], axis=-1)
    a = jnp.tile(a, (1, LANE // ROPE_AXIS_DIM))
    sign = jnp.where(jnp.arange(LANE) < LANE // 2, -1.0, 1.0).astype(F32)
    return jnp.cos(a), jnp.sin(a) * sign


def _identity_rope(seq):
    return jnp.ones((seq, LANE), F32), jnp.zeros((seq, LANE), F32)


def _pair_halves(w):
    half = ROPE_AXIS_DIM // 2
    shp = w.shape
    groups = shp[-1] // (4 * half)
    wg = w.reshape(shp[:-1] + (groups, 2, 2, half))
    first = wg[..., 0, :].reshape(shp[:-1] + (groups, 2 * half))
    second = wg[..., 1, :].reshape(shp[:-1] + (groups, 2 * half))
    return first, second


def _rope(y, cos, sin):
    return y * cos + pltpu.roll(y, LANE // 2, 1) * sin


def _mla_proj_kernel(x_ref, mod_ref, g_ref, cos_ref, sin_ref, wd_ref, gq_ref, gkv_ref, wuq_ref,
                     wukv_ref, q_ref, k_ref, vt_ref, *, heads, q_lora, kv_lora, scale):
    m = mod_ref[0]
    g = g_ref[...]
    h = (_rms(x_ref[...], g[0:1]) * (1.0 + m[1:2]) + m[0:1]).astype(BF16)
    a = _dot(h, wd_ref[...])
    cos = cos_ref[...]
    sin = sin_ref[...]
    c0 = q_lora + kv_lora
    kr = _rope(a[:, c0:c0 + LANE], cos, sin).astype(BF16)
    cq = _rms(a[:, :q_lora], gq_ref[...]).astype(BF16)
    ckv = _rms(a[:, q_lora:c0], gkv_ref[...]).astype(BF16)
    qa = _dot(cq, wuq_ref[...])
    kv = _dot(ckv, wukv_ref[...])
    hw = heads * LANE
    for hh in range(heads):
        lo = hh * LANE
        qn = qa[:, lo:lo + LANE]
        qr = _rope(qa[:, hw + lo:hw + lo + LANE], cos, sin)
        q_ref[:, 2 * lo:2 * lo + LANE] = (qn * scale).astype(BF16)
        q_ref[:, 2 * lo + LANE:2 * lo + 2 * LANE] = (qr * scale).astype(BF16)
        k_ref[:, 2 * lo:2 * lo + LANE] = kv[:, lo:lo + LANE].astype(BF16)
        k_ref[:, 2 * lo + LANE:2 * lo + 2 * LANE] = kr
    vt_ref[...] = kv[:, hw:].T.astype(BF16)


def _mla_proj(x, mod, norm_g, cos, sin, wd, gq, gkv, wuq, wukv, layer, seq, t, heads, scale):
    n, d = x.shape
    tps = seq // t
    q_lora = gq.shape[-1]
    kv_lora = gkv.shape[-1]
    qw = heads * 2 * LANE
    return pl.pallas_call(
        functools.partial(_mla_proj_kernel, heads=heads, q_lora=q_lora, kv_lora=kv_lora, scale=scale),
        grid=(n // t,),
        in_specs=[pl.BlockSpec((t, d), lambda i: (i, 0)),
                  _mod_spec(mod, tps, d),
                  pl.BlockSpec((None, 4, d), lambda i: (layer, 0, 0)),
                  pl.BlockSpec((t, LANE), lambda i: (i % tps, 0)),
                  pl.BlockSpec((t, LANE), lambda i: (i % tps, 0)),
                  pl.BlockSpec(wd.shape, lambda i: (0, 0)),
                  pl.BlockSpec((1, q_lora), lambda i: (0, 0)),
                  pl.BlockSpec((1, kv_lora), lambda i: (0, 0)),
                  pl.BlockSpec(wuq.shape, lambda i: (0, 0)),
                  pl.BlockSpec(wukv.shape, lambda i: (0, 0))],
        out_specs=[pl.BlockSpec((t, qw), lambda i: (i, 0)),
                   pl.BlockSpec((t, qw), lambda i: (i, 0)),
                   pl.BlockSpec((None, heads * LANE, t), lambda i: (i // tps, 0, i % tps))],
        out_shape=[jax.ShapeDtypeStruct((n, qw), BF16),
                   jax.ShapeDtypeStruct((n, qw), BF16),
                   jax.ShapeDtypeStruct((n // seq, heads * LANE, seq), BF16)],
        compiler_params=_params(1),
        name="mla_proj",
    )(x, mod, norm_g, cos, sin, wd, gq, gkv, wuq, wukv)


def _diff_proj_kernel(x_ref, mod_ref, g_ref, cos_ref, sin_ref, w_ref, *out_refs, need_q, scale):
    m = mod_ref[0]
    g = g_ref[...]
    d = x_ref.shape[-1]
    h = (_rms(x_ref[...], g[0:1]) * (1.0 + m[1:2]) + m[0:1]).astype(BF16)
    cos = cos_ref[...]
    sin = sin_ref[...]

    def roped(col0, ref, mul):
        y = _dot(h, w_ref[:, col0:col0 + d])
        for j in range(d // LANE):
            lo = j * LANE
            r = _rope(y[:, lo:lo + LANE], cos, sin)
            ref[:, lo:lo + LANE] = (r * mul).astype(BF16) if mul != 1.0 else r.astype(BF16)

    if need_q:
        q_ref, k_ref, vt_ref = out_refs
        roped(0, q_ref, scale)
    else:
        k_ref, vt_ref = out_refs
    roped(d, k_ref, 1.0)
    vt_ref[...] = _dot(h, w_ref[:, 2 * d:3 * d]).T.astype(BF16)


def _diff_proj(x, mod, norm_g, cos, sin, w_ext, layer, seq, t, need_q, scale):
    n, d = x.shape
    tps = seq // t
    n_out = 3 if need_q else 2
    return pl.pallas_call(
        functools.partial(_diff_proj_kernel, need_q=need_q, scale=scale),
        grid=(n // t,),
        in_specs=[pl.BlockSpec((t, d), lambda i: (i, 0)),
                  _mod_spec(mod, tps, d),
                  pl.BlockSpec((None, 4, d), lambda i: (layer, 0, 0)),
                  pl.BlockSpec((t, LANE), lambda i: (i % tps, 0)),
                  pl.BlockSpec((t, LANE), lambda i: (i % tps, 0)),
                  pl.BlockSpec(w_ext.shape, lambda i: (0, 0))],
        out_specs=[pl.BlockSpec((t, d), lambda i: (i, 0))] * (n_out - 1)
        + [pl.BlockSpec((None, d, t), lambda i: (i // tps, 0, i % tps))],
        out_shape=[jax.ShapeDtypeStruct((n, d), BF16)] * (n_out - 1)
        + [jax.ShapeDtypeStruct((n // seq, d, seq), BF16)],
        compiler_params=_params(1),
        name="diff_proj",
    )(x, mod, norm_g, cos, sin, w_ext)


def _attn_kernel(*refs, kind, n_pieces, heads, lam_init):
    x_ref, mod_ref, g_ref, q_ref = refs[:4]
    kv_refs = refs[4:4 + 2 * n_pieces]
    k_refs = kv_refs[0::2]
    vt_refs = kv_refs[1::2]
    rest = refs[4 + 2 * n_pieces:]
    if kind == "diff":
        lam_ref, gs_ref, wo_ref, o_ref, oh_ref = rest[:5]
        lv = lam_ref[...]
        lam = (jnp.exp(jnp.sum(lv[0:1] * lv[1:2], axis=-1, keepdims=True))
               - jnp.exp(jnp.sum(lv[2:3] * lv[3:4], axis=-1, keepdims=True)) + lam_init)
    else:
        wo_ref, o_ref, oh_ref = rest[:3]
    ve_refs = rest[-n_pieces:]
    m = mod_ref[0]
    g = g_ref[...]
    tq = q_ref.shape[0]
    ver = LANE + HALO_BF16

    @pl.when(pl.program_id(1) == 0)
    def _():
        for vt_ref, ve_ref in zip(vt_refs, ve_refs):
            ones = jnp.ones((HALO_BF16, vt_ref.shape[-1]), BF16)
            for hh in range(heads):
                ve_ref[hh * ver:hh * ver + LANE, :] = vt_ref[hh * LANE:(hh + 1) * LANE, :]
                ve_ref[hh * ver + LANE:(hh + 1) * ver, :] = ones

    def scores(hh):
        if kind == "mla":
            kcols = slice(hh * 2 * LANE, (hh + 1) * 2 * LANE)
            qs = q_ref[:, kcols]
        else:
            kcols = slice(hh * LANE, (hh + 1) * LANE)
            qh = q_ref[:, kcols]
            lane = lax.broadcasted_iota(jnp.int32, (tq, LANE), 1)
            in_map0 = lax.rem(lane, 2 * ROPE_AXIS_DIM) < ROPE_AXIS_DIM
            zero = jnp.zeros_like(qh)
            qs = jnp.concatenate([jnp.where(in_map0, qh, zero), jnp.where(in_map0, zero, qh)], axis=0)
        return [_dot_nt(k_ref[:, kcols], qs) for k_ref in k_refs]

    def weighted_values(st, hh):
        mx = functools.reduce(jnp.maximum, [jnp.max(si, axis=0, keepdims=True) for si in st])
        return functools.reduce(jnp.add, [_dot(ve_ref[hh * ver:(hh + 1) * ver, :],
                                               jnp.exp2(si - mx).astype(BF16))
                                          for si, ve_ref in zip(st, ve_refs)])

    st_next = scores(0)
    for hh in range(heads):
        ocols = slice(hh * LANE, (hh + 1) * LANE)
        st = st_next
        if hh + 1 < heads:
            st_next = scores(hh + 1)
        oe = weighted_values(st, hh)
        if kind == "mla":
            ot = oe[:LANE] * (1.0 / oe[LANE:LANE + 1])
        else:
            ot = (oe[:LANE, :tq] * (1.0 / oe[LANE:LANE + 1, :tq])
                  - oe[:LANE, tq:] * (lam / oe[LANE:LANE + 1, tq:]))
            ot = (ot * lax.rsqrt(jnp.mean(ot * ot, axis=0, keepdims=True) + NORM_EPS) * gs_ref[...]
                  * (1.0 - lam_init))
        oh_ref[:, ocols] = ot.T.astype(BF16)
    y = _dot(oh_ref[...], wo_ref[...])
    o_ref[...] = x_ref[...] + m[2:3] * _rms(y, g[1:2])


def _attn(x, mod, norm_g, q, pieces, wo, layer, seq, tq, kind, heads, extra=(), lam_init=0.0):
    n, d = x.shape
    tps = seq // tq
    qw = q.shape[-1]
    in_specs = [pl.BlockSpec((tq, d), lambda b, i: (b * tps + i, 0)),
                (pl.BlockSpec((1, 6, d), lambda b, i: (0, 0, 0)) if mod.shape[0] == 1
                 else pl.BlockSpec((1, 6, d), lambda b, i: (b, 0, 0))),
                pl.BlockSpec((None, 4, d), lambda b, i: (layer, 0, 0)),
                pl.BlockSpec((tq, qw), lambda b, i: (b * tps + i, 0))]
    args = [x, mod, norm_g, q]
    n_seq = n // seq
    scratch = [pltpu.VMEM((tq, d), BF16)]
    for kp, vtp in pieces:
        lk = vtp.shape[-1]
        in_specs += [pl.BlockSpec((lk, kp.shape[-1]), lambda b, i: (b, 0)),
                     pl.BlockSpec((None, vtp.shape[1], lk), lambda b, i: (b, 0, 0))]
        args += [kp, vtp]
        scratch.append(pltpu.VMEM((heads * (LANE + HALO_BF16), lk), BF16))
    for e in extra:
        in_specs.append(pl.BlockSpec(e.shape, lambda b, i: (0, 0)))
        args.append(e)
    in_specs.append(_resident(wo.shape, lambda b, i: (0, 0)))
    args.append(wo)
    return pl.pallas_call(
        functools.partial(_attn_kernel, kind=kind, n_pieces=len(pieces), heads=heads, lam_init=lam_init),
        grid=(n_seq, tps),
        in_specs=in_specs,
        out_specs=pl.BlockSpec((tq, d), lambda b, i: (b * tps + i, 0)),
        out_shape=jax.ShapeDtypeStruct((n, d), F32),
        scratch_shapes=scratch,
        compiler_params=_params(2),
        name="attn_" + kind,
    )(*args)


def kernel(x, c, ctx, c_ctx, ada_w, ada_b, norm_g, ffn_w1, ffn_w2, pool_w, pool_scale, conv_in_w, conv_w,
           conv_out_w, mla_w_down, mla_g_q, mla_g_kv, mla_w_uq, mla_w_ukv, mla_w_o, diff_w_qkv,
           diff_lambda, diff_g_subln, diff_w_o):
    bsz, seq, d = x.shape
    lc = ctx.shape[1]
    depth = ada_w.shape[0]
    n_mixers = 4

    rows = bsz + 1
    rows_pad = -(-rows // 32) * 32
    s = jnp.concatenate([c, c_ctx[None, :], jnp.zeros((rows_pad - rows, d), F32)], axis=0)
    mod_all = _ada(s, ada_w, ada_b).reshape(depth, rows_pad, 6, d)

    w1 = ffn_w1.astype(BF16)
    w2 = ffn_w2.astype(BF16)
    xl = x.reshape(bsz * seq, d)
    xc = ctx.reshape(bsz * lc, d)
    t_lat = min(ROW_TILE_LATENT, seq)
    t_ctx = min(ROW_TILE_CONTEXT, lc)
    tq_mla = min(SOFTMAX_ROWS, seq)
    tq_diff = min(SOFTMAX_ROWS // 2, seq)

    for i in range(depth):
        kind = i % n_mixers
        j = i // n_mixers
        ctx_out = i < depth - 1
        mod_l = mod_all[i, :bsz]
        mod_c = mod_all[i, bsz:bsz + 1]
        if kind == 0:
            pw = pool_w[j].astype(BF16)
            ps = pool_scale[j][None, :]
            xl = _pool(xl, mod_l, norm_g, pw, ps, i, seq, t_lat)
            if ctx_out:
                xc = _pool(xc, mod_c, norm_g, pw, ps, i, lc, t_ctx)
        elif kind == 1:
            w_in = conv_in_w[j].astype(BF16)
            w_out = conv_out_w[j].astype(BF16)
            xl = _conv(xl, mod_l, norm_g, w_in, conv_w[j], w_out, i, seq, t_lat)
            if ctx_out:
                xc = _conv(xc, mod_c, norm_g, w_in, conv_w[j], w_out, i, lc, t_ctx)
        elif kind == 2:
            heads = mla_w_uq.shape[-1] // (MLA_NOPE + MLA_ROPE)
            q_lora = mla_g_q.shape[-1]
            kv_lora = mla_g_kv.shape[-1]
            scale = (MLA_NOPE + MLA_ROPE) ** -0.5 * LOG2E
            wdn = mla_w_down[j]
            kr1, kr2 = _pair_halves(wdn[:, q_lora + kv_lora:])
            zk = jnp.zeros_like(kr1)
            wd = jnp.concatenate([wdn[:, :q_lora + kv_lora],
                                  jnp.concatenate([kr1, zk, kr2, zk], axis=-1).reshape(d, LANE)],
                                 axis=-1).astype(BF16)
            wq = mla_w_uq[j].reshape(q_lora, heads, MLA_NOPE + MLA_ROPE)
            qr1, qr2 = _pair_halves(wq[:, :, MLA_NOPE:])
            zq = jnp.zeros_like(qr1)
            wuq = jnp.concatenate([
                wq[:, :, :MLA_NOPE].reshape(q_lora, heads * MLA_NOPE),
                jnp.concatenate([qr1, zq, qr2, zq], axis=-1).reshape(q_lora, heads * LANE)],
                axis=-1).astype(BF16)
            wkv = mla_w_ukv[j].reshape(kv_lora, heads, MLA_NOPE + MLA_V)
            wukv = jnp.concatenate([wkv[:, :, :MLA_NOPE].reshape(kv_lora, heads * MLA_NOPE),
                                    wkv[:, :, MLA_NOPE:].reshape(kv_lora, heads * MLA_V)],
                                   axis=-1).astype(BF16)
            wo = mla_w_o[j].astype(BF16)
            gq = mla_g_q[j][None, :]
            gkv = mla_g_kv[j][None, :]
            cos_l, sin_l = _rope_table(seq)
            cos_c, sin_c = _identity_rope(lc)
            ql, kl, vl = _mla_proj(xl, mod_l, norm_g, cos_l, sin_l, wd, gq, gkv, wuq, wukv, i, seq, t_lat,
                                   heads, scale)
            qc, kc, vc = _mla_proj(xc, mod_c, norm_g, cos_c, sin_c, wd, gq, gkv, wuq, wukv, i, lc, t_ctx,
                                   heads, scale)
            xl = _attn(xl, mod_l, norm_g, ql, [(kl, vl), (kc, vc)], wo, i, seq, tq_mla, "mla", heads)
            if ctx_out:
                xc = _attn(xc, mod_c, norm_g, qc, [(kc, vc)], wo, i, lc, t_ctx, "mla", heads)
        else:
            heads = d // (2 * DIFF_HEAD_DIM)
            lam_init = 0.8 - 0.6 * math.exp(-0.3 * i)
            scale = DIFF_HEAD_DIM ** -0.5 * LOG2E
            wqkv = diff_w_qkv[j]
            wq, wk, wv = wqkv[:, :d], wqkv[:, d:2 * d], wqkv[:, 2 * d:]

            def paired(w):
                first, second = _pair_halves(w.reshape(d, heads, 2 * DIFF_HEAD_DIM))
                return jnp.concatenate([first.reshape(d, heads, DIFF_HEAD_DIM),
                                        second.reshape(d, heads, DIFF_HEAD_DIM)], axis=-1).reshape(d, d)

            w_ext = jnp.concatenate([paired(wq), paired(wk), wv], axis=-1).astype(BF16)
            wo = diff_w_o[j].astype(BF16)
            cos_l, sin_l = _rope_table(seq)
            cos_c, sin_c = _identity_rope(lc)
            ql, kl, vl = _diff_proj(xl, mod_l, norm_g, cos_l, sin_l, w_ext, i, seq, t_lat, True, scale)
            outs_c = _diff_proj(xc, mod_c, norm_g, cos_c, sin_c, w_ext, i, lc, t_ctx, ctx_out, scale)
            kc, vc = outs_c[-2], outs_c[-1]
            extra = (diff_lambda[j], diff_g_subln[j][:, None])
            xl = _attn(xl, mod_l, norm_g, ql, [(kl, vl), (kc, vc)], wo, i, seq, tq_diff, "diff", heads,
                       extra, lam_init)
            if ctx_out:
                xc = _attn(xc, mod_c, norm_g, outs_c[0], [(kc, vc)], wo, i, lc, t_ctx, "diff", heads,
                           extra, lam_init)
        xl = _ffn(xl, mod_l, norm_g, w1, w2, i)
        if ctx_out:
            xc = _ffn(xc, mod_c, norm_g, w1, w2, i)
    return xl.reshape(bsz, seq, d)
```

```python
import functools
import math

import jax
import jax.numpy as jnp
from jax import lax
from jax.experimental import pallas as pl
from jax.experimental.pallas import tpu as pltpu

F32 = jnp.float32
BF16 = jnp.bfloat16

GRID_W = 64
ROPE_BASE = 10000.0
NORM_EPS = 1e-6
LOG2E = math.log2(math.e)
POOL_WINDOWS = (2, 4, 8, 16)
CONV_WIDTH = 3
MLA_NOPE = 128
MLA_ROPE = 64
MLA_V = 128
DIFF_HEAD_DIM = 64
ROPE_AXIS_DIM = 32
LANE = 128
HALO_F32 = 8
HALO_BF16 = 16
VMEM_LIMIT = 60 * 1024 * 1024
ROW_TILE_LATENT = 512
ROW_TILE_CONTEXT = 256
QUERY_TILE = 512
FFN_ROW_TILE = 512
FFN_HIDDEN_CHUNK = 512


def _params(n_axes):
    return pltpu.CompilerParams(dimension_semantics=("arbitrary",) * n_axes,
                                vmem_limit_bytes=VMEM_LIMIT)


def _rms(x, g):
    return x * lax.rsqrt(jnp.mean(x * x, axis=-1, keepdims=True) + NORM_EPS) * g


def _dot(a, b):
    return jnp.dot(a, b, preferred_element_type=F32)


def _dot_nt(a, b):
    return lax.dot_general(a, b, (((1,), (1,)), ((), ())), preferred_element_type=F32)


def _ada_kernel(s_ref, w_ref, b_ref, o_ref):
    s = s_ref[...]
    a = (s / (1.0 + jnp.exp(-s))).astype(BF16)
    o_ref[...] = _dot(a, w_ref[...].astype(BF16)) + b_ref[...]


def _ada(s, ada_w, ada_b, tn=1536):
    depth, d, n = ada_w.shape
    rows = s.shape[0]
    return pl.pallas_call(
        _ada_kernel,
        grid=(depth, n // tn),
        in_specs=[pl.BlockSpec((rows, d), lambda l, j: (0, 0)),
                  pl.BlockSpec((None, d, tn), lambda l, j: (l, 0, j)),
                  pl.BlockSpec((None, 1, tn), lambda l, j: (l, 0, j))],
        out_specs=pl.BlockSpec((None, rows, tn), lambda l, j: (l, 0, j)),
        out_shape=jax.ShapeDtypeStruct((depth, rows, n), F32),
        compiler_params=_params(2),
        name="ada",
    )(s, ada_w, ada_b.reshape(depth, 1, n))


def _ffn_kernel(x_ref, mod_ref, g_ref, w1_ref, w2_ref, o_ref, h1_ref, *, fc):
    m = mod_ref[0]
    g = g_ref[...]
    x = x_ref[...]
    hf = (_rms(x, g[2:3]) * (1.0 + m[4:5]) + m[3:4]).astype(BF16)
    for c in range(h1_ref.shape[-1] // fc):
        cols = slice(c * fc, (c + 1) * fc)
        h1_ref[:, cols] = jnp.square(jnp.maximum(_dot(hf, w1_ref[:, cols]), 0.0)).astype(BF16)
    y = _dot(h1_ref[...], w2_ref[...])
    o_ref[...] = x + m[5:6] * _rms(y, g[3:4])


def _resident(block_shape, index_map):
    return pl.BlockSpec(block_shape, index_map, pipeline_mode=pl.Buffered(1))


def _ffn(x, mod, norm_g, w1, w2, layer, tm=FFN_ROW_TILE, fc=FFN_HIDDEN_CHUNK):
    n, d = x.shape
    dff = w1.shape[-1]
    tm = min(tm, n)
    n_tiles = n // tm
    tpm = n_tiles // mod.shape[0]
    return pl.pallas_call(
        functools.partial(_ffn_kernel, fc=fc),
        grid=(n_tiles,),
        in_specs=[pl.BlockSpec((tm, d), lambda i: (i, 0)),
                  pl.BlockSpec((1, 6, d), lambda i: (i // tpm, 0, 0)),
                  pl.BlockSpec((None, 4, d), lambda i: (layer, 0, 0)),
                  _resident((None, d, dff), lambda i: (layer, 0, 0)),
                  _resident((None, dff, d), lambda i: (layer, 0, 0))],
        out_specs=pl.BlockSpec((tm, d), lambda i: (i, 0)),
        out_shape=jax.ShapeDtypeStruct((n, d), F32),
        scratch_shapes=[pltpu.VMEM((tm, dff), BF16)],
        compiler_params=_params(1),
        name="ffn",
    )(x, mod, norm_g, w1, w2)


def _halo_specs(t, halo, n, d):
    r = t // halo
    last = n // halo - 1
    return [pl.BlockSpec((t, d), lambda i: (i, 0)),
            pl.BlockSpec((halo, d), lambda i: (jnp.maximum(i * r - 1, 0), 0)),
            pl.BlockSpec((halo, d), lambda i: (jnp.minimum((i + 1) * r, last), 0))]


def _mod_spec(mod, tps, d):
    if mod.shape[0] == 1:
        return pl.BlockSpec((1, 6, d), lambda i: (0, 0, 0))
    return pl.BlockSpec((1, 6, d), lambda i: (i // tps, 0, 0))


def _pool_kernel(x_ref, xp_ref, xn_ref, mod_ref, g_ref, pw_ref, ps_ref, o_ref, hs_ref, *, t, seq, tps):
    i = pl.program_id(0)
    tile = lax.rem(i, tps)
    first = tile == 0
    last = tile == tps - 1
    m = mod_ref[0]
    g = g_ref[...]

    def hmod(xx):
        return _rms(xx, g[0:1]) * (1.0 + m[1:2]) + m[0:1]

    x = x_ref[...]
    h = hmod(x)
    hb = HALO_F32
    hs_ref[0:hb, :] = jnp.where(first, 0.0, hmod(xp_ref[...]))
    hs_ref[hb:hb + t, :] = h
    hs_ref[hb + t:2 * hb + t, :] = jnp.where(last, 0.0, hmod(xn_ref[...]))
    pos = tile * t + lax.broadcasted_iota(jnp.int32, (t, 1), 0)
    grp = pw_ref.shape[-1]
    ys = []
    for gi, w in enumerate(POOL_WINDOWS):
        lo = w // 2
        hi = w - 1 - lo
        cols = slice(gi * grp, (gi + 1) * grp)
        rows = t + 2 * hb
        run = hs_ref[:, cols]
        span = 1
        while span < w:
            run = run + pltpu.roll(run, rows - span, 0)
            span *= 2
        start = hb - lo
        acc = (pltpu.roll(run, rows - start, 0) if start else run)[:t]
        cnt = (jnp.minimum(pos + hi + 1, seq) - jnp.maximum(pos - lo, 0)).astype(F32)
        dlt = acc * (1.0 / cnt) - h[:, cols]
        ys.append(_dot(dlt.astype(BF16), pw_ref[gi]))
    y = jnp.concatenate(ys, axis=-1) * ps_ref[...]
    o_ref[...] = x + m[2:3] * _rms(y, g[1:2])


def _pool(x, mod, norm_g, pool_w, pool_scale, layer, seq, t):
    n, d = x.shape
    tps = seq // t
    ng, grp, _ = pool_w.shape
    return pl.pallas_call(
        functools.partial(_pool_kernel, t=t, seq=seq, tps=tps),
        grid=(n // t,),
        in_specs=_halo_specs(t, HALO_F32, n, d) + [
            _mod_spec(mod, tps, d),
            pl.BlockSpec((None, 4, d), lambda i: (layer, 0, 0)),
            pl.BlockSpec((ng, grp, grp), lambda i: (0, 0, 0)),
            pl.BlockSpec((1, d), lambda i: (0, 0))],
        out_specs=pl.BlockSpec((t, d), lambda i: (i, 0)),
        out_shape=jax.ShapeDtypeStruct((n, d), F32),
        scratch_shapes=[pltpu.VMEM((t + 2 * HALO_F32, d), F32)],
        compiler_params=_params(1),
        name="pool",
    )(x, x, x, mod, norm_g, pool_w, pool_scale)


def _conv_kernel(x_ref, xp_ref, xn_ref, mod_ref, g_ref, win_ref, cw_ref, wout_ref, o_ref,
                 hs_ref, u_ref, *, t, tps):
    i = pl.program_id(0)
    tile = lax.rem(i, tps)
    first = tile == 0
    last = tile == tps - 1
    m = mod_ref[0]
    g = g_ref[...]
    d = x_ref.shape[-1]

    def hmod(xx):
        return _rms(xx, g[0:1]) * (1.0 + m[1:2]) + m[0:1]

    x = x_ref[...]
    hb = HALO_BF16
    hs_ref[0:hb, :] = jnp.where(first, 0.0, hmod(xp_ref[...])).astype(BF16)
    hs_ref[hb:hb + t, :] = hmod(x).astype(BF16)
    hs_ref[hb + t:2 * hb + t, :] = jnp.where(last, 0.0, hmod(xn_ref[...])).astype(BF16)
    hs = hs_ref[...]
    u_ref[...] = _dot(hs, win_ref[:, d:2 * d]) * _dot(hs, win_ref[:, 2 * d:3 * d])
    cw = cw_ref[...]
    z = (u_ref[hb - 1:hb - 1 + t, :] * cw[0:1] + u_ref[hb:hb + t, :] * cw[1:2]
         + u_ref[hb + 1:hb + 1 + t, :] * cw[2:3])
    b_gate = _dot(hs_ref[hb:hb + t, :], win_ref[:, 0:d])
    y = _dot((b_gate * z).astype(BF16), wout_ref[...])
    o_ref[...] = x + m[2:3] * _rms(y, g[1:2])


def _conv(x, mod, norm_g, w_in, conv_w, w_out, layer, seq, t):
    n, d = x.shape
    tps = seq // t
    return pl.pallas_call(
        functools.partial(_conv_kernel, t=t, tps=tps),
        grid=(n // t,),
        in_specs=_halo_specs(t, HALO_BF16, n, d) + [
            _mod_spec(mod, tps, d),
            pl.BlockSpec((None, 4, d), lambda i: (layer, 0, 0)),
            pl.BlockSpec((d, 3 * d), lambda i: (0, 0)),
            pl.BlockSpec((CONV_WIDTH, d), lambda i: (0, 0)),
            pl.BlockSpec((d, d), lambda i: (0, 0))],
        out_specs=pl.BlockSpec((t, d), lambda i: (i, 0)),
        out_shape=jax.ShapeDtypeStruct((n, d), F32),
        scratch_shapes=[pltpu.VMEM((t + 2 * HALO_BF16, d), BF16),
                        pltpu.VMEM((t + 2 * HALO_BF16, d), F32)],
        compiler_params=_params(1),
        name="conv",
    )(x, x, x, mod, norm_g, w_in, conv_w, w_out)


def _rope_table(seq):
    tpos = jnp.arange(seq)
    row = (tpos // GRID_W).astype(F32)
    col = (tpos % GRID_W).astype(F32)
    inv_freq = ROPE_BASE ** (-jnp.arange(0, ROPE_AXIS_DIM, 2, dtype=F32) / ROPE_AXIS_DIM)
    a = jnp.concatenate([row[:, None] * inv_freq[None, :], col[:, None] * inv_freq[None, :]], axis=-1)
    a = jnp.tile(a, (1, LANE // ROPE_AXIS_DIM))
    sign = jnp.where(jnp.arange(LANE) < LANE // 2, -1.0, 1.0).astype(F32)
    return jnp.cos(a), jnp.sin(a) * sign


def _identity_rope(seq):
    return jnp.ones((seq, LANE), F32), jnp.zeros((seq, LANE), F32)


def _pair_halves(w):
    half = ROPE_AXIS_DIM // 2
    shp = w.shape
    groups = shp[-1] // (4 * half)
    wg = w.reshape(shp[:-1] + (groups, 2, 2, half))
    first = wg[..., 0, :].reshape(shp[:-1] + (groups, 2 * half))
    second = wg[..., 1, :].reshape(shp[:-1] + (groups, 2 * half))
    return first, second


def _rope(y, cos, sin):
    return y * cos + pltpu.roll(y, LANE // 2, 1) * sin


def _mla_proj_kernel(x_ref, mod_ref, g_ref, cos_ref, sin_ref, wd_ref, gq_ref, gkv_ref, wuq_ref,
                     wukv_ref, q_ref, k_ref, vt_ref, *, heads, q_lora, kv_lora, scale):
    m = mod_ref[0]
    g = g_ref[...]
    h = (_rms(x_ref[...], g[0:1]) * (1.0 + m[1:2]) + m[0:1]).astype(BF16)
    a = _dot(h, wd_ref[...])
    cos = cos_ref[...]
    sin = sin_ref[...]
    c0 = q_lora + kv_lora
    kr = _rope(a[:, c0:c0 + LANE], cos, sin).astype(BF16)
    cq = _rms(a[:, :q_lora], gq_ref[...]).astype(BF16)
    ckv = _rms(a[:, q_lora:c0], gkv_ref[...]).astype(BF16)
    qa = _dot(cq, wuq_ref[...])
    kv = _dot(ckv, wukv_ref[...])
    hw = heads * LANE
    for hh in range(heads):
        lo = hh * LANE
        qn = qa[:, lo:lo + LANE]
        qr = _rope(qa[:, hw + lo:hw + lo + LANE], cos, sin)
        q_ref[:, 2 * lo:2 * lo + LANE] = (qn * scale).astype(BF16)
        q_ref[:, 2 * lo + LANE:2 * lo + 2 * LANE] = (qr * scale).astype(BF16)
        k_ref[:, 2 * lo:2 * lo + LANE] = kv[:, lo:lo + LANE].astype(BF16)
        k_ref[:, 2 * lo + LANE:2 * lo + 2 * LANE] = kr
    vt_ref[...] = kv[:, hw:].T.astype(BF16)


def _mla_proj(x, mod, norm_g, cos, sin, wd, gq, gkv, wuq, wukv, layer, seq, t, heads, scale):
    n, d = x.shape
    tps = seq // t
    q_lora = gq.shape[-1]
    kv_lora = gkv.shape[-1]
    qw = heads * 2 * LANE
    return pl.pallas_call(
        functools.partial(_mla_proj_kernel, heads=heads, q_lora=q_lora, kv_lora=kv_lora, scale=scale),
        grid=(n // t,),
        in_specs=[pl.BlockSpec((t, d), lambda i: (i, 0)),
                  _mod_spec(mod, tps, d),
                  pl.BlockSpec((None, 4, d), lambda i: (layer, 0, 0)),
                  pl.BlockSpec((t, LANE), lambda i: (i % tps, 0)),
                  pl.BlockSpec((t, LANE), lambda i: (i % tps, 0)),
                  pl.BlockSpec(wd.shape, lambda i: (0, 0)),
                  pl.BlockSpec((1, q_lora), lambda i: (0, 0)),
                  pl.BlockSpec((1, kv_lora), lambda i: (0, 0)),
                  pl.BlockSpec(wuq.shape, lambda i: (0, 0)),
                  pl.BlockSpec(wukv.shape, lambda i: (0, 0))],
        out_specs=[pl.BlockSpec((t, qw), lambda i: (i, 0)),
                   pl.BlockSpec((t, qw), lambda i: (i, 0)),
                   pl.BlockSpec((None, heads * LANE, t), lambda i: (i // tps, 0, i % tps))],
        out_shape=[jax.ShapeDtypeStruct((n, qw), BF16),
                   jax.ShapeDtypeStruct((n, qw), BF16),
                   jax.ShapeDtypeStruct((n // seq, heads * LANE, seq), BF16)],
        compiler_params=_params(1),
        name="mla_proj",
    )(x, mod, norm_g, cos, sin, wd, gq, gkv, wuq, wukv)


def _diff_proj_kernel(x_ref, mod_ref, g_ref, cos_ref, sin_ref, w_ref, *out_refs, need_q, scale):
    m = mod_ref[0]
    g = g_ref[...]
    d = x_ref.shape[-1]
    h = (_rms(x_ref[...], g[0:1]) * (1.0 + m[1:2]) + m[0:1]).astype(BF16)
    cos = cos_ref[...]
    sin = sin_ref[...]

    def roped(col0, ref, mul):
        y = _dot(h, w_ref[:, col0:col0 + d])
        for j in range(d // LANE):
            lo = j * LANE
            r = _rope(y[:, lo:lo + LANE], cos, sin)
            ref[:, lo:lo + LANE] = (r * mul).astype(BF16) if mul != 1.0 else r.astype(BF16)

    if need_q:
        q_ref, k_ref, vt_ref = out_refs
        roped(0, q_ref, scale)
    else:
        k_ref, vt_ref = out_refs
    roped(d, k_ref, 1.0)
    vt_ref[...] = _dot(h, w_ref[:, 2 * d:3 * d]).T.astype(BF16)


def _diff_proj(x, mod, norm_g, cos, sin, w_ext, layer, seq, t, need_q, scale):
    n, d = x.shape
    tps = seq // t
    n_out = 3 if need_q else 2
    return pl.pallas_call(
        functools.partial(_diff_proj_kernel, need_q=need_q, scale=scale),
        grid=(n // t,),
        in_specs=[pl.BlockSpec((t, d), lambda i: (i, 0)),
                  _mod_spec(mod, tps, d),
                  pl.BlockSpec((None, 4, d), lambda i: (layer, 0, 0)),
                  pl.BlockSpec((t, LANE), lambda i: (i % tps, 0)),
                  pl.BlockSpec((t, LANE), lambda i: (i % tps, 0)),
                  pl.BlockSpec(w_ext.shape, lambda i: (0, 0))],
        out_specs=[pl.BlockSpec((t, d), lambda i: (i, 0))] * (n_out - 1)
        + [pl.BlockSpec((None, d, t), lambda i: (i // tps, 0, i % tps))],
        out_shape=[jax.ShapeDtypeStruct((n, d), BF16)] * (n_out - 1)
        + [jax.ShapeDtypeStruct((n // seq, d, seq), BF16)],
        compiler_params=_params(1),
        name="diff_proj",
    )(x, mod, norm_g, cos, sin, w_ext)


def _attn_kernel(*refs, kind, n_pieces, heads, lam_init):
    x_ref, mod_ref, g_ref, q_ref = refs[:4]
    kv_refs = refs[4:4 + 2 * n_pieces]
    k_refs = kv_refs[0::2]
    vt_refs = kv_refs[1::2]
    rest = refs[4 + 2 * n_pieces:]
    if kind == "diff":
        lam_ref, gs_ref, wo_ref, o_ref, oh_ref = rest[:5]
        lv = lam_ref[...]
        lam = (jnp.exp(jnp.sum(lv[0:1] * lv[1:2], axis=-1, keepdims=True))
               - jnp.exp(jnp.sum(lv[2:3] * lv[3:4], axis=-1, keepdims=True)) + lam_init)
    else:
        wo_ref, o_ref, oh_ref = rest[:3]
    ve_refs = rest[-n_pieces:]
    m = mod_ref[0]
    g = g_ref[...]
    tq = q_ref.shape[0]
    ver = LANE + HALO_BF16

    @pl.when(pl.program_id(1) == 0)
    def _():
        for vt_ref, ve_ref in zip(vt_refs, ve_refs):
            ones = jnp.ones((HALO_BF16, vt_ref.shape[-1]), BF16)
            for hh in range(heads):
                ve_ref[hh * ver:hh * ver + LANE, :] = vt_ref[hh * LANE:(hh + 1) * LANE, :]
                ve_ref[hh * ver + LANE:(hh + 1) * ver, :] = ones

    def scores(hh):
        if kind == "mla":
            kcols = slice(hh * 2 * LANE, (hh + 1) * 2 * LANE)
            qs = q_ref[:, kcols]
        else:
            kcols = slice(hh * LANE, (hh + 1) * LANE)
            qh = q_ref[:, kcols]
            lane = lax.broadcasted_iota(jnp.int32, (tq, LANE), 1)
            in_map0 = lax.rem(lane, 2 * ROPE_AXIS_DIM) < ROPE_AXIS_DIM
            zero = jnp.zeros_like(qh)
            qs = jnp.concatenate([jnp.where(in_map0, qh, zero), jnp.where(in_map0, zero, qh)], axis=0)
        st = [_dot_nt(k_ref[:, kcols], qs) for k_ref in k_refs]
        return st, functools.reduce(jnp.maximum, [jnp.max(si, axis=0, keepdims=True) for si in st])

    def weighted_values(st, mx, hh):
        return functools.reduce(jnp.add, [_dot(ve_ref[hh * ver:(hh + 1) * ver, :],
                                               jnp.exp2(si - mx).astype(BF16))
                                          for si, ve_ref in zip(st, ve_refs)])

    nxt = scores(0)
    for hh in range(heads):
        ocols = slice(hh * LANE, (hh + 1) * LANE)
        st, mx = nxt
        if hh + 1 < heads:
            nxt = scores(hh + 1)
        oe = weighted_values(st, mx, hh)
        if kind == "mla":
            ot = oe[:LANE] * (1.0 / oe[LANE:LANE + 1])
        else:
            ot = (oe[:LANE, :tq] * (1.0 / oe[LANE:LANE + 1, :tq])
                  - oe[:LANE, tq:] * (lam / oe[LANE:LANE + 1, tq:]))
            ot = (ot * lax.rsqrt(jnp.mean(ot * ot, axis=0, keepdims=True) + NORM_EPS) * gs_ref[...]
                  * (1.0 - lam_init))
        oh_ref[:, ocols] = ot.T.astype(BF16)
    y = _dot(oh_ref[...], wo_ref[...])
    o_ref[...] = x_ref[...] + m[2:3] * _rms(y, g[1:2])


def _attn(x, mod, norm_g, q, pieces, wo, layer, seq, tq, kind, heads, extra=(), lam_init=0.0):
    n, d = x.shape
    tps = seq // tq
    qw = q.shape[-1]
    in_specs = [pl.BlockSpec((tq, d), lambda b, i: (b * tps + i, 0)),
                (pl.BlockSpec((1, 6, d), lambda b, i: (0, 0, 0)) if mod.shape[0] == 1
                 else pl.BlockSpec((1, 6, d), lambda b, i: (b, 0, 0))),
                pl.BlockSpec((None, 4, d), lambda b, i: (layer, 0, 0)),
                pl.BlockSpec((tq, qw), lambda b, i: (b * tps + i, 0))]
    args = [x, mod, norm_g, q]
    n_seq = n // seq
    scratch = [pltpu.VMEM((tq, d), BF16)]
    kv_mode = pl.Buffered(1) if kind == "diff" else None
    for kp, vtp in pieces:
        lk = vtp.shape[-1]
        in_specs += [pl.BlockSpec((lk, kp.shape[-1]), lambda b, i: (b, 0), pipeline_mode=kv_mode),
                     pl.BlockSpec((None, vtp.shape[1], lk), lambda b, i: (b, 0, 0), pipeline_mode=kv_mode)]
        args += [kp, vtp]
        scratch.append(pltpu.VMEM((heads * (LANE + HALO_BF16), lk), BF16))
    for e in extra:
        in_specs.append(pl.BlockSpec(e.shape, lambda b, i: (0, 0)))
        args.append(e)
    in_specs.append(_resident(wo.shape, lambda b, i: (0, 0)))
    args.append(wo)
    return pl.pallas_call(
        functools.partial(_attn_kernel, kind=kind, n_pieces=len(pieces), heads=heads, lam_init=lam_init),
        grid=(n_seq, tps),
        in_specs=in_specs,
        out_specs=pl.BlockSpec((tq, d), lambda b, i: (b * tps + i, 0)),
        out_shape=jax.ShapeDtypeStruct((n, d), F32),
        scratch_shapes=scratch,
        compiler_params=_params(2),
        name="attn_" + kind,
    )(*args)


def kernel(x, c, ctx, c_ctx, ada_w, ada_b, norm_g, ffn_w1, ffn_w2, pool_w, pool_scale, conv_in_w, conv_w,
           conv_out_w, mla_w_down, mla_g_q, mla_g_kv, mla_w_uq, mla_w_ukv, mla_w_o, diff_w_qkv,
           diff_lambda, diff_g_subln, diff_w_o):
    bsz, seq, d = x.shape
    lc = ctx.shape[1]
    depth = ada_w.shape[0]
    n_mixers = 4

    rows = bsz + 1
    rows_pad = -(-rows // 32) * 32
    s = jnp.concatenate([c, c_ctx[None, :], jnp.zeros((rows_pad - rows, d), F32)], axis=0)
    mod_all = _ada(s, ada_w, ada_b).reshape(depth, rows_pad, 6, d)

    w1 = ffn_w1.astype(BF16)
    w2 = ffn_w2.astype(BF16)
    xl = x.reshape(bsz * seq, d)
    xc = ctx.reshape(bsz * lc, d)
    t_lat = min(ROW_TILE_LATENT, seq)
    t_ctx = min(ROW_TILE_CONTEXT, lc)
    tq = min(QUERY_TILE, seq)

    for i in range(depth):
        kind = i % n_mixers
        j = i // n_mixers
        ctx_out = i < depth - 1
        mod_l = mod_all[i, :bsz]
        mod_c = mod_all[i, bsz:bsz + 1]
        if kind == 0:
            pw = pool_w[j].astype(BF16)
            ps = pool_scale[j][None, :]
            xl = _pool(xl, mod_l, norm_g, pw, ps, i, seq, t_lat)
            if ctx_out:
                xc = _pool(xc, mod_c, norm_g, pw, ps, i, lc, t_ctx)
        elif kind == 1:
            w_in = conv_in_w[j].astype(BF16)
            w_out = conv_out_w[j].astype(BF16)
            xl = _conv(xl, mod_l, norm_g, w_in, conv_w[j], w_out, i, seq, t_lat)
            if ctx_out:
                xc = _conv(xc, mod_c, norm_g, w_in, conv_w[j], w_out, i, lc, t_ctx)
        elif kind == 2:
            heads = mla_w_uq.shape[-1] // (MLA_NOPE + MLA_ROPE)
            q_lora = mla_g_q.shape[-1]
            kv_lora = mla_g_kv.shape[-1]
            scale = (MLA_NOPE + MLA_ROPE) ** -0.5 * LOG2E
            wdn = mla_w_down[j]
            kr1, kr2 = _pair_halves(wdn[:, q_lora + kv_lora:])
            zk = jnp.zeros_like(kr1)
            wd = jnp.concatenate([wdn[:, :q_lora + kv_lora],
                                  jnp.concatenate([kr1, zk, kr2, zk], axis=-1).reshape(d, LANE)],
                                 axis=-1).astype(BF16)
            wq = mla_w_uq[j].reshape(q_lora, heads, MLA_NOPE + MLA_ROPE)
            qr1, qr2 = _pair_halves(wq[:, :, MLA_NOPE:])
            zq = jnp.zeros_like(qr1)
            wuq = jnp.concatenate([
                wq[:, :, :MLA_NOPE].reshape(q_lora, heads * MLA_NOPE),
                jnp.concatenate([qr1, zq, qr2, zq], axis=-1).reshape(q_lora, heads * LANE)],
                axis=-1).astype(BF16)
            wkv = mla_w_ukv[j].reshape(kv_lora, heads, MLA_NOPE + MLA_V)
            wukv = jnp.concatenate([wkv[:, :, :MLA_NOPE].reshape(kv_lora, heads * MLA_NOPE),
                                    wkv[:, :, MLA_NOPE:].reshape(kv_lora, heads * MLA_V)],
                                   axis=-1).astype(BF16)
            wo = mla_w_o[j].astype(BF16)
            gq = mla_g_q[j][None, :]
            gkv = mla_g_kv[j][None, :]
            cos_l, sin_l = _rope_table(seq)
            cos_c, sin_c = _identity_rope(lc)
            ql, kl, vl = _mla_proj(xl, mod_l, norm_g, cos_l, sin_l, wd, gq, gkv, wuq, wukv, i, seq, t_lat,
                                   heads, scale)
            qc, kc, vc = _mla_proj(xc, mod_c, norm_g, cos_c, sin_c, wd, gq, gkv, wuq, wukv, i, lc, t_ctx,
                                   heads, scale)
            xl = _attn(xl, mod_l, norm_g, ql, [(kl, vl), (kc, vc)], wo, i, seq, tq, "mla", heads)
            if ctx_out:
                xc = _attn(xc, mod_c, norm_g, qc, [(kc, vc)], wo, i, lc, t_ctx, "mla", heads)
        else:
            heads = d // (2 * DIFF_HEAD_DIM)
            lam_init = 0.8 - 0.6 * math.exp(-0.3 * i)
            scale = DIFF_HEAD_DIM ** -0.5 * LOG2E
            wqkv = diff_w_qkv[j]
            wq, wk, wv = wqkv[:, :d], wqkv[:, d:2 * d], wqkv[:, 2 * d:]

            def paired(w):
                first, second = _pair_halves(w.reshape(d, heads, 2 * DIFF_HEAD_DIM))
                return jnp.concatenate([first.reshape(d, heads, DIFF_HEAD_DIM),
                                        second.reshape(d, heads, DIFF_HEAD_DIM)], axis=-1).reshape(d, d)

            w_ext = jnp.concatenate([paired(wq), paired(wk), wv], axis=-1).astype(BF16)
            wo = diff_w_o[j].astype(BF16)
            cos_l, sin_l = _rope_table(seq)
            cos_c, sin_c = _identity_rope(lc)
            ql, kl, vl = _diff_proj(xl, mod_l, norm_g, cos_l, sin_l, w_ext, i, seq, t_lat, True, scale)
            outs_c = _diff_proj(xc, mod_c, norm_g, cos_c, sin_c, w_ext, i, lc, t_ctx, ctx_out, scale)
            kc, vc = outs_c[-2], outs_c[-1]
            extra = (diff_lambda[j], diff_g_subln[j][:, None])
            xl = _attn(xl, mod_l, norm_g, ql, [(kl, vl), (kc, vc)], wo, i, seq, tq, "diff", heads,
                       extra, lam_init)
            if ctx_out:
                xc = _attn(xc, mod_c, norm_g, outs_c[0], [(kc, vc)], wo, i, lc, t_ctx, "diff", heads,
                           extra, lam_init)
        xl = _ffn(xl, mod_l, norm_g, w1, w2, i)
        if ctx_out:
            xc = _ffn(xc, mod_c, norm_g, w1, w2, i)
    return xl.reshape(bsz, seq, d)
```

```python
import functools
import math

import jax
import jax.numpy as jnp
from jax import lax
from jax.experimental import pallas as pl
from jax.experimental.pallas import tpu as pltpu

F32 = jnp.float32
BF16 = jnp.bfloat16

GRID_W = 64
ROPE_BASE = 10000.0
NORM_EPS = 1e-6
LOG2E = math.log2(math.e)
POOL_WINDOWS = (2, 4, 8, 16)
CONV_WIDTH = 3
MLA_NOPE = 128
MLA_ROPE = 64
MLA_V = 128
DIFF_HEAD_DIM = 64
ROPE_AXIS_DIM = 32
LANE = 128
HALO_F32 = 8
HALO_BF16 = 16
VMEM_LIMIT = 60 * 1024 * 1024
ROW_TILE_LATENT = 512
ROW_TILE_CONTEXT = 256
QUERY_TILE = 512
FFN_ROW_TILE = 512
FFN_HIDDEN_CHUNK = 512


def _params(n_axes):
    return pltpu.CompilerParams(dimension_semantics=("arbitrary",) * n_axes,
                                vmem_limit_bytes=VMEM_LIMIT)


def _rms(x, g):
    return x * lax.rsqrt(jnp.mean(x * x, axis=-1, keepdims=True) + NORM_EPS) * g


def _dot(a, b):
    return jnp.dot(a, b, preferred_element_type=F32)


def _dot_nt(a, b):
    return lax.dot_general(a, b, (((1,), (1,)), ((), ())), preferred_element_type=F32)


def _ada_kernel(s_ref, w_ref, b_ref, o_ref):
    s = s_ref[...]
    a = (s / (1.0 + jnp.exp(-s))).astype(BF16)
    o_ref[...] = _dot(a, w_ref[...].astype(BF16)) + b_ref[...]


def _ada(s, ada_w, ada_b, tn=1536):
    depth, d, n = ada_w.shape
    rows = s.shape[0]
    return pl.pallas_call(
        _ada_kernel,
        grid=(depth, n // tn),
        in_specs=[pl.BlockSpec((rows, d), lambda l, j: (0, 0)),
                  pl.BlockSpec((None, d, tn), lambda l, j: (l, 0, j)),
                  pl.BlockSpec((None, 1, tn), lambda l, j: (l, 0, j))],
        out_specs=pl.BlockSpec((None, rows, tn), lambda l, j: (l, 0, j)),
        out_shape=jax.ShapeDtypeStruct((depth, rows, n), F32),
        compiler_params=_params(2),
        name="ada",
    )(s, ada_w, ada_b.reshape(depth, 1, n))


def _ffn_kernel(x_ref, mod_ref, g_ref, w1_ref, w2_ref, o_ref, h1_ref, *, fc):
    m = mod_ref[0]
    g = g_ref[...]
    x = x_ref[...]
    hf = (_rms(x, g[2:3]) * (1.0 + m[4:5]) + m[3:4]).astype(BF16)
    for c in range(h1_ref.shape[-1] // fc):
        cols = slice(c * fc, (c + 1) * fc)
        h1_ref[:, cols] = jnp.square(jnp.maximum(_dot(hf, w1_ref[:, cols]), 0.0)).astype(BF16)
    y = _dot(h1_ref[...], w2_ref[...])
    o_ref[...] = x + m[5:6] * _rms(y, g[3:4])


def _resident(block_shape, index_map):
    return pl.BlockSpec(block_shape, index_map, pipeline_mode=pl.Buffered(1))


def _ffn(x, mod, norm_g, w1, w2, layer, tm=FFN_ROW_TILE, fc=FFN_HIDDEN_CHUNK):
    n, d = x.shape
    dff = w1.shape[-1]
    tm = min(tm, n)
    n_tiles = n // tm
    tpm = n_tiles // mod.shape[0]
    return pl.pallas_call(
        functools.partial(_ffn_kernel, fc=fc),
        grid=(n_tiles,),
        in_specs=[pl.BlockSpec((tm, d), lambda i: (i, 0)),
                  pl.BlockSpec((1, 6, d), lambda i: (i // tpm, 0, 0)),
                  pl.BlockSpec((None, 4, d), lambda i: (layer, 0, 0)),
                  _resident((None, d, dff), lambda i: (layer, 0, 0)),
                  _resident((None, dff, d), lambda i: (layer, 0, 0))],
        out_specs=pl.BlockSpec((tm, d), lambda i: (i, 0)),
        out_shape=jax.ShapeDtypeStruct((n, d), F32),
        scratch_shapes=[pltpu.VMEM((tm, dff), BF16)],
        compiler_params=_params(1),
        name="ffn",
    )(x, mod, norm_g, w1, w2)


def _halo_specs(t, halo, n, d):
    r = t // halo
    last = n // halo - 1
    return [pl.BlockSpec((t, d), lambda i: (i, 0)),
            pl.BlockSpec((halo, d), lambda i: (jnp.maximum(i * r - 1, 0), 0)),
            pl.BlockSpec((halo, d), lambda i: (jnp.minimum((i + 1) * r, last), 0))]


def _mod_spec(mod, tps, d):
    if mod.shape[0] == 1:
        return pl.BlockSpec((1, 6, d), lambda i: (0, 0, 0))
    return pl.BlockSpec((1, 6, d), lambda i: (i // tps, 0, 0))


def _pool_kernel(x_ref, xp_ref, xn_ref, mod_ref, g_ref, pw_ref, ps_ref, o_ref, hs_ref, *, t, seq, tps):
    i = pl.program_id(0)
    tile = lax.rem(i, tps)
    first = tile == 0
    last = tile == tps - 1
    m = mod_ref[0]
    g = g_ref[...]

    def hmod(xx):
        return _rms(xx, g[0:1]) * (1.0 + m[1:2]) + m[0:1]

    x = x_ref[...]
    h = hmod(x)
    hb = HALO_F32
    hs_ref[0:hb, :] = jnp.where(first, 0.0, hmod(xp_ref[...]))
    hs_ref[hb:hb + t, :] = h
    hs_ref[hb + t:2 * hb + t, :] = jnp.where(last, 0.0, hmod(xn_ref[...]))
    pos = tile * t + lax.broadcasted_iota(jnp.int32, (t, 1), 0)
    grp = pw_ref.shape[-1]
    ys = []
    for gi, w in enumerate(POOL_WINDOWS):
        lo = w // 2
        hi = w - 1 - lo
        cols = slice(gi * grp, (gi + 1) * grp)
        rows = t + 2 * hb
        run = hs_ref[:, cols]
        span = 1
        while span < w:
            run = run + pltpu.roll(run, rows - span, 0)
            span *= 2
        start = hb - lo
        acc = (pltpu.roll(run, rows - start, 0) if start else run)[:t]
        cnt = (jnp.minimum(pos + hi + 1, seq) - jnp.maximum(pos - lo, 0)).astype(F32)
        dlt = acc * (1.0 / cnt) - h[:, cols]
        ys.append(_dot(dlt.astype(BF16), pw_ref[gi]))
    y = jnp.concatenate(ys, axis=-1) * ps_ref[...]
    o_ref[...] = x + m[2:3] * _rms(y, g[1:2])


def _pool(x, mod, norm_g, pool_w, pool_scale, layer, seq, t):
    n, d = x.shape
    tps = seq // t
    ng, grp, _ = pool_w.shape
    return pl.pallas_call(
        functools.partial(_pool_kernel, t=t, seq=seq, tps=tps),
        grid=(n // t,),
        in_specs=_halo_specs(t, HALO_F32, n, d) + [
            _mod_spec(mod, tps, d),
            pl.BlockSpec((None, 4, d), lambda i: (layer, 0, 0)),
            pl.BlockSpec((ng, grp, grp), lambda i: (0, 0, 0)),
            pl.BlockSpec((1, d), lambda i: (0, 0))],
        out_specs=pl.BlockSpec((t, d), lambda i: (i, 0)),
        out_shape=jax.ShapeDtypeStruct((n, d), F32),
        scratch_shapes=[pltpu.VMEM((t + 2 * HALO_F32, d), F32)],
        compiler_params=_params(1),
        name="pool",
    )(x, x, x, mod, norm_g, pool_w, pool_scale)


def _conv_kernel(x_ref, xp_ref, xn_ref, mod_ref, g_ref, win_ref, cw_ref, wout_ref, o_ref,
                 hs_ref, u_ref, *, t, tps):
    i = pl.program_id(0)
    tile = lax.rem(i, tps)
    first = tile == 0
    last = tile == tps - 1
    m = mod_ref[0]
    g = g_ref[...]
    d = x_ref.shape[-1]

    def hmod(xx):
        return _rms(xx, g[0:1]) * (1.0 + m[1:2]) + m[0:1]

    x = x_ref[...]
    hb = HALO_BF16
    hs_ref[0:hb, :] = jnp.where(first, 0.0, hmod(xp_ref[...])).astype(BF16)
    hs_ref[hb:hb + t, :] = hmod(x).astype(BF16)
    hs_ref[hb + t:2 * hb + t, :] = jnp.where(last, 0.0, hmod(xn_ref[...])).astype(BF16)
    hs = hs_ref[...]
    u_ref[...] = _dot(hs, win_ref[:, d:2 * d]) * _dot(hs, win_ref[:, 2 * d:3 * d])
    cw = cw_ref[...]
    z = (u_ref[hb - 1:hb - 1 + t, :] * cw[0:1] + u_ref[hb:hb + t, :] * cw[1:2]
         + u_ref[hb + 1:hb + 1 + t, :] * cw[2:3])
    b_gate = _dot(hs_ref[hb:hb + t, :], win_ref[:, 0:d])
    y = _dot((b_gate * z).astype(BF16), wout_ref[...])
    o_ref[...] = x + m[2:3] * _rms(y, g[1:2])


def _conv(x, mod, norm_g, w_in, conv_w, w_out, layer, seq, t):
    n, d = x.shape
    tps = seq // t
    return pl.pallas_call(
        functools.partial(_conv_kernel, t=t, tps=tps),
        grid=(n // t,),
        in_specs=_halo_specs(t, HALO_BF16, n, d) + [
            _mod_spec(mod, tps, d),
            pl.BlockSpec((None, 4, d), lambda i: (layer, 0, 0)),
            pl.BlockSpec((d, 3 * d), lambda i: (0, 0)),
            pl.BlockSpec((CONV_WIDTH, d), lambda i: (0, 0)),
            pl.BlockSpec((d, d), lambda i: (0, 0))],
        out_specs=pl.BlockSpec((t, d), lambda i: (i, 0)),
        out_shape=jax.ShapeDtypeStruct((n, d), F32),
        scratch_shapes=[pltpu.VMEM((t + 2 * HALO_BF16, d), BF16),
                        pltpu.VMEM((t + 2 * HALO_BF16, d), F32)],
        compiler_params=_params(1),
        name="conv",
    )(x, x, x, mod, norm_g, w_in, conv_w, w_out)


def _rope_table(seq):
    tpos = jnp.arange(seq)
    row = (tpos // GRID_W).astype(F32)
    col = (tpos % GRID_W).astype(F32)
    inv_freq = ROPE_BASE ** (-jnp.arange(0, ROPE_AXIS_DIM, 2, dtype=F32) / ROPE_AXIS_DIM)
    a = jnp.concatenate([row[:, None] * inv_freq[None, :], col[:, None] * inv_freq[None, :]], axis=-1)
    a = jnp.tile(a, (1, LANE // ROPE_AXIS_DIM))
    sign = jnp.where(jnp.arange(LANE) < LANE // 2, -1.0, 1.0).astype(F32)
    return jnp.cos(a), jnp.sin(a) * sign


def _identity_rope(seq):
    return jnp.ones((seq, LANE), F32), jnp.zeros((seq, LANE), F32)


def _pair_halves(w):
    half = ROPE_AXIS_DIM // 2
    shp = w.shape
    groups = shp[-1] // (4 * half)
    wg = w.reshape(shp[:-1] + (groups, 2, 2, half))
    first = wg[..., 0, :].reshape(shp[:-1] + (groups, 2 * half))
    second = wg[..., 1, :].reshape(shp[:-1] + (groups, 2 * half))
    return first, second


def _rope(y, cos, sin):
    return y * cos + pltpu.roll(y, LANE // 2, 1) * sin


def _mla_proj_kernel(x_ref, mod_ref, g_ref, cos_ref, sin_ref, wd_ref, gq_ref, gkv_ref, wuq_ref,
                     wukv_ref, q_ref, k_ref, vt_ref, *, heads, q_lora, kv_lora, scale):
    m = mod_ref[0]
    g = g_ref[...]
    h = (_rms(x_ref[...], g[0:1]) * (1.0 + m[1:2]) + m[0:1]).astype(BF16)
    a = _dot(h, wd_ref[...])
    cos = cos_ref[...]
    sin = sin_ref[...]
    c0 = q_lora + kv_lora
    kr = _rope(a[:, c0:c0 + LANE], cos, sin).astype(BF16)
    cq = _rms(a[:, :q_lora], gq_ref[...]).astype(BF16)
    ckv = _rms(a[:, q_lora:c0], gkv_ref[...]).astype(BF16)
    qa = _dot(cq, wuq_ref[...])
    kv = _dot(ckv, wukv_ref[...])
    hw = heads * LANE
    for hh in range(heads):
        lo = hh * LANE
        qn = qa[:, lo:lo + LANE]
        qr = _rope(qa[:, hw + lo:hw + lo + LANE], cos, sin)
        q_ref[:, 2 * lo:2 * lo + LANE] = (qn * scale).astype(BF16)
        q_ref[:, 2 * lo + LANE:2 * lo + 2 * LANE] = (qr * scale).astype(BF16)
        k_ref[:, 2 * lo:2 * lo + LANE] = kv[:, lo:lo + LANE].astype(BF16)
        k_ref[:, 2 * lo + LANE:2 * lo + 2 * LANE] = kr
    vt_ref[...] = kv[:, hw:].T.astype(BF16)


def _mla_proj(x, mod, norm_g, cos, sin, wd, gq, gkv, wuq, wukv, layer, seq, t, heads, scale):
    n, d = x.shape
    tps = seq // t
    q_lora = gq.shape[-1]
    kv_lora = gkv.shape[-1]
    qw = heads * 2 * LANE
    return pl.pallas_call(
        functools.partial(_mla_proj_kernel, heads=heads, q_lora=q_lora, kv_lora=kv_lora, scale=scale),
        grid=(n // t,),
        in_specs=[pl.BlockSpec((t, d), lambda i: (i, 0)),
                  _mod_spec(mod, tps, d),
                  pl.BlockSpec((None, 4, d), lambda i: (layer, 0, 0)),
                  pl.BlockSpec((t, LANE), lambda i: (i % tps, 0)),
                  pl.BlockSpec((t, LANE), lambda i: (i % tps, 0)),
                  pl.BlockSpec(wd.shape, lambda i: (0, 0)),
                  pl.BlockSpec((1, q_lora), lambda i: (0, 0)),
                  pl.BlockSpec((1, kv_lora), lambda i: (0, 0)),
                  pl.BlockSpec(wuq.shape, lambda i: (0, 0)),
                  pl.BlockSpec(wukv.shape, lambda i: (0, 0))],
        out_specs=[pl.BlockSpec((t, qw), lambda i: (i, 0)),
                   pl.BlockSpec((t, qw), lambda i: (i, 0)),
                   pl.BlockSpec((None, heads * LANE, t), lambda i: (i // tps, 0, i % tps))],
        out_shape=[jax.ShapeDtypeStruct((n, qw), BF16),
                   jax.ShapeDtypeStruct((n, qw), BF16),
                   jax.ShapeDtypeStruct((n // seq, heads * LANE, seq), BF16)],
        compiler_params=_params(1),
        name="mla_proj",
    )(x, mod, norm_g, cos, sin, wd, gq, gkv, wuq, wukv)


def _diff_proj_kernel(x_ref, mod_ref, g_ref, cos_ref, sin_ref, w_ref, *out_refs, need_q, scale):
    m = mod_ref[0]
    g = g_ref[...]
    d = x_ref.shape[-1]
    h = (_rms(x_ref[...], g[0:1]) * (1.0 + m[1:2]) + m[0:1]).astype(BF16)
    cos = cos_ref[...]
    sin = sin_ref[...]

    def roped(col0, ref, mul):
        y = _dot(h, w_ref[:, col0:col0 + d])
        for j in range(d // LANE):
            lo = j * LANE
            r = _rope(y[:, lo:lo + LANE], cos, sin)
            ref[:, lo:lo + LANE] = (r * mul).astype(BF16) if mul != 1.0 else r.astype(BF16)

    if need_q:
        q_ref, k_ref, vt_ref = out_refs
        roped(0, q_ref, scale)
    else:
        k_ref, vt_ref = out_refs
    roped(d, k_ref, 1.0)
    vt_ref[...] = _dot(h, w_ref[:, 2 * d:3 * d]).T.astype(BF16)


def _diff_proj(x, mod, norm_g, cos, sin, w_ext, layer, seq, t, need_q, scale):
    n, d = x.shape
    tps = seq // t
    n_out = 3 if need_q else 2
    return pl.pallas_call(
        functools.partial(_diff_proj_kernel, need_q=need_q, scale=scale),
        grid=(n // t,),
        in_specs=[pl.BlockSpec((t, d), lambda i: (i, 0)),
                  _mod_spec(mod, tps, d),
                  pl.BlockSpec((None, 4, d), lambda i: (layer, 0, 0)),
                  pl.BlockSpec((t, LANE), lambda i: (i % tps, 0)),
                  pl.BlockSpec((t, LANE), lambda i: (i % tps, 0)),
                  pl.BlockSpec(w_ext.shape, lambda i: (0, 0))],
        out_specs=[pl.BlockSpec((t, d), lambda i: (i, 0))] * (n_out - 1)
        + [pl.BlockSpec((None, d, t), lambda i: (i // tps, 0, i % tps))],
        out_shape=[jax.ShapeDtypeStruct((n, d), BF16)] * (n_out - 1)
        + [jax.ShapeDtypeStruct((n // seq, d, seq), BF16)],
        compiler_params=_params(1),
        name="diff_proj",
    )(x, mod, norm_g, cos, sin, w_ext)


def _attn_kernel(*refs, kind, n_pieces, heads, lam_init):
    x_ref, mod_ref, g_ref, q_ref = refs[:4]
    kv_refs = refs[4:4 + 2 * n_pieces]
    k_refs = kv_refs[0::2]
    vt_refs = kv_refs[1::2]
    rest = refs[4 + 2 * n_pieces:]
    if kind == "diff":
        lam_ref, gs_ref, wo_ref, o_ref, oh_ref = rest[:5]
        lv = lam_ref[...]
        lam = (jnp.exp(jnp.sum(lv[0:1] * lv[1:2], axis=-1, keepdims=True))
               - jnp.exp(jnp.sum(lv[2:3] * lv[3:4], axis=-1, keepdims=True)) + lam_init)
    else:
        wo_ref, o_ref, oh_ref = rest[:3]
    ve_refs = rest[-n_pieces:]
    m = mod_ref[0]
    g = g_ref[...]
    tq = q_ref.shape[0]
    ver = LANE + HALO_BF16

    @pl.when(pl.program_id(1) == 0)
    def _():
        for vt_ref, ve_ref in zip(vt_refs, ve_refs):
            ones = jnp.ones((HALO_BF16, vt_ref.shape[-1]), BF16)
            for hh in range(heads):
                ve_ref[hh * ver:hh * ver + LANE, :] = vt_ref[hh * LANE:(hh + 1) * LANE, :]
                ve_ref[hh * ver + LANE:(hh + 1) * ver, :] = ones

    def scores(hh):
        if kind == "mla":
            kcols = slice(hh * 2 * LANE, (hh + 1) * 2 * LANE)
            qs = q_ref[:, kcols]
        else:
            kcols = slice(hh * LANE, (hh + 1) * LANE)
            qh = q_ref[:, kcols]
            lane = lax.broadcasted_iota(jnp.int32, (tq, LANE), 1)
            in_map0 = lax.rem(lane, 2 * ROPE_AXIS_DIM) < ROPE_AXIS_DIM
            zero = jnp.zeros_like(qh)
            qs = jnp.concatenate([jnp.where(in_map0, qh, zero), jnp.where(in_map0, zero, qh)], axis=0)
        st = [_dot_nt(k_ref[:, kcols], qs) for k_ref in k_refs]
        return st, functools.reduce(jnp.maximum, [jnp.max(si, axis=0, keepdims=True) for si in st])

    def weighted_values(st, mx, hh):
        return functools.reduce(jnp.add, [_dot(ve_ref[hh * ver:(hh + 1) * ver, :],
                                               jnp.exp2(si - mx).astype(BF16))
                                          for si, ve_ref in zip(st, ve_refs)])

    nxt = scores(0)
    for hh in range(heads):
        ocols = slice(hh * LANE, (hh + 1) * LANE)
        st, mx = nxt
        if hh + 1 < heads:
            nxt = scores(hh + 1)
        oe = weighted_values(st, mx, hh)
        if kind == "mla":
            ot = oe[:LANE] * (1.0 / oe[LANE:LANE + 1])
        else:
            ot = (oe[:LANE, :tq] * (1.0 / oe[LANE:LANE + 1, :tq])
                  - oe[:LANE, tq:] * (lam / oe[LANE:LANE + 1, tq:]))
            ot = (ot * lax.rsqrt(jnp.mean(ot * ot, axis=0, keepdims=True) + NORM_EPS) * gs_ref[...]
                  * (1.0 - lam_init))
        oh_ref[:, ocols] = ot.T.astype(BF16)
    y = _dot(oh_ref[...], wo_ref[...])
    o_ref[...] = x_ref[...] + m[2:3] * _rms(y, g[1:2])


def _attn(x, mod, norm_g, q, pieces, wo, layer, seq, tq, kind, heads, extra=(), lam_init=0.0):
    n, d = x.shape
    tps = seq // tq
    qw = q.shape[-1]
    in_specs = [pl.BlockSpec((tq, d), lambda b, i: (b * tps + i, 0)),
                (pl.BlockSpec((1, 6, d), lambda b, i: (0, 0, 0)) if mod.shape[0] == 1
                 else pl.BlockSpec((1, 6, d), lambda b, i: (b, 0, 0))),
                pl.BlockSpec((None, 4, d), lambda b, i: (layer, 0, 0)),
                pl.BlockSpec((tq, qw), lambda b, i: (b * tps + i, 0))]
    args = [x, mod, norm_g, q]
    n_seq = n // seq
    scratch = [pltpu.VMEM((tq, d), BF16)]
    for kp, vtp in pieces:
        lk = vtp.shape[-1]
        in_specs += [pl.BlockSpec((lk, kp.shape[-1]), lambda b, i: (b, 0)),
                     pl.BlockSpec((None, vtp.shape[1], lk), lambda b, i: (b, 0, 0))]
        args += [kp, vtp]
        scratch.append(pltpu.VMEM((heads * (LANE + HALO_BF16), lk), BF16))
    for e in extra:
        in_specs.append(pl.BlockSpec(e.shape, lambda b, i: (0, 0)))
        args.append(e)
    in_specs.append(_resident(wo.shape, lambda b, i: (0, 0)))
    args.append(wo)
    return pl.pallas_call(
        functools.partial(_attn_kernel, kind=kind, n_pieces=len(pieces), heads=heads, lam_init=lam_init),
        grid=(n_seq, tps),
        in_specs=in_specs,
        out_specs=pl.BlockSpec((tq, d), lambda b, i: (b * tps + i, 0)),
        out_shape=jax.ShapeDtypeStruct((n, d), F32),
        scratch_shapes=scratch,
        compiler_params=_params(2),
        name="attn_" + kind,
    )(*args)


def kernel(x, c, ctx, c_ctx, ada_w, ada_b, norm_g, ffn_w1, ffn_w2, pool_w, pool_scale, conv_in_w, conv_w,
           conv_out_w, mla_w_down, mla_g_q, mla_g_kv, mla_w_uq, mla_w_ukv, mla_w_o, diff_w_qkv,
           diff_lambda, diff_g_subln, diff_w_o):
    bsz, seq, d = x.shape
    lc = ctx.shape[1]
    depth = ada_w.shape[0]
    n_mixers = 4

    rows = bsz + 1
    rows_pad = -(-rows // 32) * 32
    s = jnp.concatenate([c, c_ctx[None, :], jnp.zeros((rows_pad - rows, d), F32)], axis=0)
    mod_all = _ada(s, ada_w, ada_b).reshape(depth, rows_pad, 6, d)

    w1 = ffn_w1.astype(BF16)
    w2 = ffn_w2.astype(BF16)
    xl = x.reshape(bsz * seq, d)
    xc = ctx.reshape(bsz * lc, d)
    t_lat = min(ROW_TILE_LATENT, seq)
    t_ctx = min(ROW_TILE_CONTEXT, lc)
    tq = min(QUERY_TILE, seq)

    for i in range(depth):
        kind = i % n_mixers
        j = i // n_mixers
        ctx_out = i < depth - 1
        mod_l = mod_all[i, :bsz]
        mod_c = mod_all[i, bsz:bsz + 1]
        if kind == 0:
            pw = pool_w[j].astype(BF16)
            ps = pool_scale[j][None, :]
            xl = _pool(xl, mod_l, norm_g, pw, ps, i, seq, t_lat)
            if ctx_out:
                xc = _pool(xc, mod_c, norm_g, pw, ps, i, lc, t_ctx)
        elif kind == 1:
            w_in = conv_in_w[j].astype(BF16)
            w_out = conv_out_w[j].astype(BF16)
            xl = _conv(xl, mod_l, norm_g, w_in, conv_w[j], w_out, i, seq, t_lat)
            if ctx_out:
                xc = _conv(xc, mod_c, norm_g, w_in, conv_w[j], w_out, i, lc, t_ctx)
        elif kind == 2:
            heads = mla_w_uq.shape[-1] // (MLA_NOPE + MLA_ROPE)
            q_lora = mla_g_q.shape[-1]
            kv_lora = mla_g_kv.shape[-1]
            scale = (MLA_NOPE + MLA_ROPE) ** -0.5 * LOG2E
            wdn = mla_w_down[j]
            kr1, kr2 = _pair_halves(wdn[:, q_lora + kv_lora:])
            zk = jnp.zeros_like(kr1)
            wd = jnp.concatenate([wdn[:, :q_lora + kv_lora],
                                  jnp.concatenate([kr1, zk, kr2, zk], axis=-1).reshape(d, LANE)],
                                 axis=-1).astype(BF16)
            wq = mla_w_uq[j].reshape(q_lora, heads, MLA_NOPE + MLA_ROPE)
            qr1, qr2 = _pair_halves(wq[:, :, MLA_NOPE:])
            zq = jnp.zeros_like(qr1)
            wuq = jnp.concatenate([
                wq[:, :, :MLA_NOPE].reshape(q_lora, heads * MLA_NOPE),
                jnp.concatenate([qr1, zq, qr2, zq], axis=-1).reshape(q_lora, heads * LANE)],
                axis=-1).astype(BF16)
            wkv = mla_w_ukv[j].reshape(kv_lora, heads, MLA_NOPE + MLA_V)
            wukv = jnp.concatenate([wkv[:, :, :MLA_NOPE].reshape(kv_lora, heads * MLA_NOPE),
                                    wkv[:, :, MLA_NOPE:].reshape(kv_lora, heads * MLA_V)],
                                   axis=-1).astype(BF16)
            wo = mla_w_o[j].astype(BF16)
            gq = mla_g_q[j][None, :]
            gkv = mla_g_kv[j][None, :]
            cos_l, sin_l = _rope_table(seq)
            cos_c, sin_c = _identity_rope(lc)
            ql, kl, vl = _mla_proj(xl, mod_l, norm_g, cos_l, sin_l, wd, gq, gkv, wuq, wukv, i, seq, t_lat,
                                   heads, scale)
            qc, kc, vc = _mla_proj(xc, mod_c, norm_g, cos_c, sin_c, wd, gq, gkv, wuq, wukv, i, lc, t_ctx,
                                   heads, scale)
            xl = _attn(xl, mod_l, norm_g, ql, [(kl, vl), (kc, vc)], wo, i, seq, tq, "mla", heads)
            if ctx_out:
                xc = _attn(xc, mod_c, norm_g, qc, [(kc, vc)], wo, i, lc, t_ctx, "mla", heads)
        else:
            heads = d // (2 * DIFF_HEAD_DIM)
            lam_init = 0.8 - 0.6 * math.exp(-0.3 * i)
            scale = DIFF_HEAD_DIM ** -0.5 * LOG2E
            wqkv = diff_w_qkv[j]
            wq, wk, wv = wqkv[:, :d], wqkv[:, d:2 * d], wqkv[:, 2 * d:]

            def paired(w):
                first, second = _pair_halves(w.reshape(d, heads, 2 * DIFF_HEAD_DIM))
                return jnp.concatenate([first.reshape(d, heads, DIFF_HEAD_DIM),
                                        second.reshape(d, heads, DIFF_HEAD_DIM)], axis=-1).reshape(d, d)

            w_ext = jnp.concatenate([paired(wq), paired(wk), wv], axis=-1).astype(BF16)
            wo = diff_w_o[j].astype(BF16)
            cos_l, sin_l = _rope_table(seq)
            cos_c, sin_c = _identity_rope(lc)
            ql, kl, vl = _diff_proj(xl, mod_l, norm_g, cos_l, sin_l, w_ext, i, seq, t_lat, True, scale)
            outs_c = _diff_proj(xc, mod_c, norm_g, cos_c, sin_c, w_ext, i, lc, t_ctx, ctx_out, scale)
            kc, vc = outs_c[-2], outs_c[-1]
            extra = (diff_lambda[j], diff_g_subln[j][:, None])
            xl = _attn(xl, mod_l, norm_g, ql, [(kl, vl), (kc, vc)], wo, i, seq, tq, "diff", heads,
                       extra, lam_init)
            if ctx_out:
                xc = _attn(xc, mod_c, norm_g, outs_c[0], [(kc, vc)], wo, i, lc, t_ctx, "diff", heads,
                           extra, lam_init)
        xl = _ffn(xl, mod_l, norm_g, w1, w2, i)
        if ctx_out:
            xc = _ffn(xc, mod_c, norm_g, w1, w2, i)
    return xl.reshape(bsz, seq, d)
```
